```python
import jax, jax.numpy as jnp
from jax import lax
import numpy as np

D_MODEL = 1024
BATCH = 4
SEQ = 8192
DEPTH = 2
DEC_BATCH = 16
DEC_SEQ = 32
PAST_LEN = 2048

CHUNK = 64
GLA_H = 6
GLA_DK = 32
GLA_DV = 64
GLA_RANK = 16
GLA_GATE_NORM = 16.0
HG_H = 5
HG_DK = 64
HG_DV = 64
ML_H = 5
ML_DK = 64
ML_DV = 64
ML_CONV = 4
D_MIX = GLA_H * GLA_DV + HG_H * HG_DV + ML_H * ML_DV
D_FF = 2816
FFN_CONV = 3
EPS = 1e-6
NEG_BIG = -1e30

SPLIT_SIZES = (GLA_H * GLA_DK, GLA_H * GLA_DK, GLA_H * GLA_DV, GLA_RANK, GLA_H * GLA_DV,
               HG_H * HG_DK, HG_H * HG_DK, HG_H * HG_DV, HG_H * HG_DV,
               ML_H * ML_DK, ML_H * ML_DK, ML_H * ML_DV, ML_H, ML_H, ML_H * ML_DV)
N_IN = sum(SPLIT_SIZES)

kernel_name = 'hybrid_gla_hgrn2_mlstm_stream_step'


def rms_norm(x, g):
    xf = x.astype(jnp.float32)
    y = xf * lax.rsqrt(jnp.mean(xf * xf, axis=-1, keepdims=True) + EPS)
    return (y * g.astype(jnp.float32)).astype(x.dtype)


def head_rms_norm(x, g):
    h, d = x.shape[-2], x.shape[-1]
    y = x * lax.rsqrt(jnp.mean(x * x, axis=-1, keepdims=True) + EPS)
    return y * g.reshape(h, d)


def causal_dwconv(x, buf, w, b):
    width, t = w.shape[0], x.shape[1]
    xp = jnp.concatenate([buf.astype(x.dtype), x], axis=1)
    y = b
    for j in range(width):
        y = y + xp[:, j:j + t] * w[j]
    return y, xp[:, t:]


def _to_chunks(a, L):
    b, t = a.shape[0], a.shape[1]
    return a.reshape((b, t // L, L) + a.shape[2:]).swapaxes(0, 1)


def _from_chunks(a):
    n, b, L = a.shape[0], a.shape[1], a.shape[2]
    return a.swapaxes(0, 1).reshape((b, n * L) + a.shape[3:])


def gated_linear_scan(q, k, v, log_a, s0, L):
    mask = jnp.tril(jnp.ones((L, L), dtype=bool))[None, :, :, None, None]

    def step(s, inp):
        qc, kc, vc, ac = inp
        b = jnp.cumsum(ac, axis=1)
        diff = b[:, :, None] - b[:, None, :]
        decay = jnp.where(mask, jnp.exp(jnp.where(mask, diff, 0.0)), 0.0)
        scores = jnp.einsum('bihd,bjhd,bijhd->bijh', qc, kc, decay)
        o = (jnp.einsum('bijh,bjhv->bihv', scores, vc)
             + jnp.einsum('bihk,bhkv->bihv', qc * jnp.exp(b), s))
        b_last = b[:, -1]
        s_new = (jnp.exp(b_last)[..., None] * s
                 + jnp.einsum('bjhk,bjhv->bhkv', kc * jnp.exp(b_last[:, None] - b), vc))
        return s_new, o

    s_fin, o = lax.scan(step, s0, tuple(_to_chunks(a, L) for a in (q, k, v, log_a)))
    return _from_chunks(o), s_fin


def mlstm_scan(q, k, v, log_f, ig, c0, n0, m0, L):
    mask = jnp.tril(jnp.ones((L, L), dtype=bool))[None, :, :, None]

    def step(carry, inp):
        c, n, m = carry
        qc, kc, vc, fc, ic = inp
        b = jnp.cumsum(fc, axis=1)
        raw = b[:, :, None] - b[:, None, :] + ic[:, None, :]
        dlog = jnp.where(mask, raw, NEG_BIG)
        g = b + m[:, None]
        m_i = jnp.maximum(g, jnp.max(dlog, axis=2))
        w = jnp.where(mask, jnp.exp(dlog - m_i[:, :, None]), 0.0)
        w0 = jnp.exp(g - m_i)
        s = jnp.einsum('bihd,bjhd->bijh', qc, kc) * w
        num = (jnp.einsum('bijh,bjhv->bihv', s, vc)
               + w0[..., None] * jnp.einsum('bihk,bhkv->bihv', qc, c))
        den = jnp.sum(s, axis=2) + w0 * jnp.einsum('bihk,bhk->bih', qc, n)
        h = num / jnp.maximum(jnp.abs(den), jnp.exp(-m_i))[..., None]
        m_last = m_i[:, -1]
        wk = jnp.exp(b[:, -1:] - b + ic - m_last[:, None])
        dc = jnp.exp(b[:, -1] + m - m_last)
        c_new = dc[..., None, None] * c + jnp.einsum('bjh,bjhk,bjhv->bhkv', wk, kc, vc)
        n_new = dc[..., None] * n + jnp.einsum('bjh,bjhk->bhk', wk, kc)
        return (c_new, n_new, m_last), h

    (c, n, m), h = lax.scan(step, (c0, n0, m0),
                            tuple(_to_chunks(a, L) for a in (q, k, v, log_f, ig)))
    return _from_chunks(h), (c, n, m)


def mixer(xn, st, lb, w_in, gla_w_gate, gla_b_gate, ml_conv_w, ml_conv_b, ml_b_i, ml_b_f,
          g_head, w_out):
    s_gla, s_hg, c_ml, n_ml, m_ml, buf_ml = st
    B, T, _ = xn.shape
    L = CHUNK if T % CHUNK == 0 else T
    f32 = jnp.float32
    pts, acc = [], 0
    for sz in SPLIT_SIZES[:-1]:
        acc += sz
        pts.append(acc)
    proj = xn @ w_in
    (gq, gk, gv, gg, gr, hq, hf, hi, hg, mq, mk, mv, mi, mf, mo) = jnp.split(proj, pts, axis=-1)

    def hs(a, h):
        return a.astype(f32).reshape(B, T, h, -1)

    g_a, g_b, g_c = jnp.split(g_head.astype(f32), [GLA_H * GLA_DV, GLA_H * GLA_DV + HG_H * HG_DV])

    q = hs(gq, GLA_H) * GLA_DK ** -0.5
    log_a = jax.nn.log_sigmoid((gg @ gla_w_gate + gla_b_gate).astype(f32)).reshape(
        B, T, GLA_H, GLA_DK) / GLA_GATE_NORM
    o_a, s_gla = gated_linear_scan(q, hs(gk, GLA_H), hs(gv, GLA_H), log_a, s_gla.astype(f32), L)
    o_a = head_rms_norm(o_a, g_a) * jax.nn.silu(hs(gr, GLA_H))

    lbh = lb.astype(f32).reshape(HG_H, HG_DK)
    fr = hs(hf, HG_H)
    log_f = jnp.log(lbh + (1.0 - lbh) * jax.nn.sigmoid(fr))
    k = (1.0 - lbh) * jax.nn.sigmoid(-fr)
    q = jax.nn.silu(hs(hq, HG_H)) * HG_DK ** -0.5
    o_b, s_hg = gated_linear_scan(q, k, hs(hi, HG_H), log_f, s_hg.astype(f32), L)
    o_b = head_rms_norm(o_b, g_b) * jax.nn.silu(hs(hg, HG_H))

    qk, buf_ml = causal_dwconv(jnp.concatenate([mq, mk], axis=-1), buf_ml, ml_conv_w, ml_conv_b)
    qk = jax.nn.silu(qk.astype(f32))
    q = qk[..., :ML_H * ML_DK].reshape(B, T, ML_H, ML_DK)
    k = qk[..., ML_H * ML_DK:].reshape(B, T, ML_H, ML_DK) * ML_DK ** -0.5
    ig = mi.astype(f32) + ml_b_i.astype(f32)
    log_fm = jax.nn.log_sigmoid(mf.astype(f32) + ml_b_f.astype(f32))
    h, (c_ml, n_ml, m_ml) = mlstm_scan(q, k, hs(mv, ML_H), log_fm, ig, c_ml.astype(f32),
                                       n_ml.astype(f32), m_ml.astype(f32), L)
    o_c = head_rms_norm(h, g_c) * jax.nn.sigmoid(hs(mo, ML_H))

    cat = jnp.concatenate([o_a.reshape(B, T, -1), o_b.reshape(B, T, -1), o_c.reshape(B, T, -1)],
                          axis=-1).astype(xn.dtype)
    return cat @ w_out, (s_gla, s_hg, c_ml, n_ml, m_ml, buf_ml)


def conv_ffn(xn, buf, w_up, conv_w, conv_b, w_down):
    up = xn @ w_up
    up, buf = causal_dwconv(up, buf, conv_w, conv_b)
    gate, val = jnp.split(up, 2, axis=-1)
    return (jax.nn.gelu(gate, approximate=True) * val) @ w_down, buf


def run_trunk(x, states, params):
    (g_mix_pre, g_mix_post, g_ffn_pre, g_ffn_post, w_in, gla_w_gate, gla_b_gate, hgrn_lb,
     ml_conv_w, ml_conv_b, ml_b_i, ml_b_f, g_head, w_out, ffn_w_up, ffn_conv_w, ffn_conv_b,
     ffn_w_down) = params
    sm = jax.nn.softmax(hgrn_lb.astype(jnp.float32), axis=0)
    lb_all = jnp.cumsum(sm, axis=0) - sm[0:1]
    new = [[] for _ in range(7)]
    for l in range(DEPTH):
        st = tuple(s[l] for s in states[:6])
        h, st_new = mixer(rms_norm(x, g_mix_pre[l]), st, lb_all[l], w_in[l], gla_w_gate[l],
                          gla_b_gate[l], ml_conv_w[l], ml_conv_b[l], ml_b_i[l], ml_b_f[l],
                          g_head[l], w_out[l])
        x = x + rms_norm(h, g_mix_post[l])
        h, ffn_buf = conv_ffn(rms_norm(x, g_ffn_pre[l]), states[6][l], ffn_w_up[l], ffn_conv_w[l],
                              ffn_conv_b[l], ffn_w_down[l])
        x = x + rms_norm(h, g_ffn_post[l])
        for i, s in enumerate(st_new + (ffn_buf,)):
            new[i].append(s)
    return x, [jnp.stack(s, axis=0) for s in new]


def setup_inputs(seed: int = 0) -> dict:
    key = jax.random.key(seed)
    ks = jax.random.split(key, 32)

    def nrm(k, shape, s):
        return s * jax.random.normal(k, shape, jnp.float32)

    def gain(k, shape):
        return 1.0 + 0.05 * jax.random.normal(k, shape, jnp.float32)

    D = D_MODEL
    return {
        'x_prompt': nrm(ks[0], (BATCH, SEQ, D), 1.0),
        'x_sample': nrm(ks[1], (DEC_BATCH, DEC_SEQ, D), 1.0),
        'state_gla': nrm(ks[2], (DEPTH, DEC_BATCH, GLA_H, GLA_DK, GLA_DV), 0.5),
        'state_hgrn': nrm(ks[3], (DEPTH, DEC_BATCH, HG_H, HG_DK, HG_DV), 1.0),
        'state_mlstm_C': nrm(ks[4], (DEPTH, DEC_BATCH, ML_H, ML_DK, ML_DV), 0.1),
        'state_mlstm_n': nrm(ks[5], (DEPTH, DEC_BATCH, ML_H, ML_DK), 0.1),
        'state_mlstm_m': nrm(ks[6], (DEPTH, DEC_BATCH, ML_H), 1.0),
        'cache_mlstm_conv': nrm(ks[7], (DEPTH, DEC_BATCH, ML_CONV - 1, 2 * ML_H * ML_DK), 1.0),
        'cache_ffn_conv': nrm(ks[8], (DEPTH, DEC_BATCH, FFN_CONV - 1, 2 * D_FF), 1.0),
        'g_mix_pre': gain(ks[9], (DEPTH, D)),
        'g_mix_post': gain(ks[10], (DEPTH, D)),
        'g_ffn_pre': gain(ks[11], (DEPTH, D)),
        'g_ffn_post': gain(ks[12], (DEPTH, D)),
        'w_in': nrm(ks[13], (DEPTH, D, N_IN), D ** -0.5),
        'gla_w_gate': nrm(ks[14], (DEPTH, GLA_RANK, GLA_H * GLA_DK), GLA_RANK ** -0.5),
        'gla_b_gate': nrm(ks[15], (DEPTH, GLA_H * GLA_DK), 0.1),
        'hgrn_lb': nrm(ks[16], (DEPTH, HG_H * HG_DK), 0.1),
        'ml_conv_w': nrm(ks[17], (DEPTH, ML_CONV, 2 * ML_H * ML_DK), ML_CONV ** -0.5),
        'ml_conv_b': nrm(ks[18], (DEPTH, 2 * ML_H * ML_DK), 0.01),
        'ml_b_i': nrm(ks[19], (DEPTH, ML_H), 0.1),
        'ml_b_f': jnp.linspace(3.0, 6.0, ML_H, dtype=jnp.float32)[None, :] + nrm(ks[20], (DEPTH, ML_H), 0.1),
        'g_head': gain(ks[21], (DEPTH, D_MIX)),
        'w_out': nrm(ks[22], (DEPTH, D_MIX, D), D_MIX ** -0.5),
        'ffn_w_up': nrm(ks[23], (DEPTH, D, 2 * D_FF), D ** -0.5),
        'ffn_conv_w': nrm(ks[24], (DEPTH, FFN_CONV, 2 * D_FF), FFN_CONV ** -0.5),
        'ffn_conv_b': nrm(ks[25], (DEPTH, 2 * D_FF), 0.01),
        'ffn_w_down': nrm(ks[26], (DEPTH, D_FF, D), D_FF ** -0.5),
    }


def reference(x_prompt, x_sample, state_gla, state_hgrn, state_mlstm_C, state_mlstm_n,
              state_mlstm_m, cache_mlstm_conv, cache_ffn_conv, g_mix_pre, g_mix_post, g_ffn_pre,
              g_ffn_post, w_in, gla_w_gate, gla_b_gate, hgrn_lb, ml_conv_w, ml_conv_b, ml_b_i,
              ml_b_f, g_head, w_out, ffn_w_up, ffn_conv_w, ffn_conv_b, ffn_w_down):
    params = (g_mix_pre, g_mix_post, g_ffn_pre, g_ffn_post, w_in, gla_w_gate, gla_b_gate, hgrn_lb,
              ml_conv_w, ml_conv_b, ml_b_i, ml_b_f, g_head, w_out, ffn_w_up, ffn_conv_w,
              ffn_conv_b, ffn_w_down)
    f32 = jnp.float32
    bp = x_prompt.shape[0]
    zero_states = (
        jnp.zeros((DEPTH, bp, GLA_H, GLA_DK, GLA_DV), f32),
        jnp.zeros((DEPTH, bp, HG_H, HG_DK, HG_DV), f32),
        jnp.zeros((DEPTH, bp, ML_H, ML_DK, ML_DV), f32),
        jnp.zeros((DEPTH, bp, ML_H, ML_DK), f32),
        jnp.zeros((DEPTH, bp, ML_H), f32),
        jnp.zeros((DEPTH, bp, ML_CONV - 1, 2 * ML_H * ML_DK), x_prompt.dtype),
        jnp.zeros((DEPTH, bp, FFN_CONV - 1, 2 * D_FF), x_prompt.dtype),
    )
    y_prompt, ps = run_trunk(x_prompt, zero_states, params)
    sample_states = (state_gla, state_hgrn, state_mlstm_C, state_mlstm_n, state_mlstm_m,
                     cache_mlstm_conv, cache_ffn_conv)
    y_sample, ss = run_trunk(x_sample, sample_states, params)
    return (y_prompt, y_sample, ps[0], ps[1], ps[2], ps[3], ps[4], ps[5], ps[6],
            ss[0], ss[1], ss[2], ss[3], ss[4], ss[5], ss[6])
```

```python
import functools

import numpy as np
import jax
import jax.numpy as jnp
from jax import lax
from jax.experimental import pallas as pl
from jax.experimental.pallas import tpu as pltpu

F32 = jnp.float32
BF16 = jnp.bfloat16
HIGHEST = lax.Precision.HIGHEST

D_MODEL = 1024
DEPTH = 2
GLA_H, GLA_DK, GLA_DV, GLA_RANK, GLA_GATE_NORM = 6, 32, 64, 16, 16.0
HG_H, HG_DK, HG_DV = 5, 64, 64
ML_H, ML_DK, ML_DV, ML_CONV = 5, 64, 64, 4
D_MIX = GLA_H * GLA_DV + HG_H * HG_DV + ML_H * ML_DV
D_FF = 2816
FFN_CONV = 3
EPS = 1e-6
NEG_BIG = -1e30

LANES = 128
SUBLANES = 8
SUB = 16
MAX_CHUNK = 64
EXP_CLAMP = 80.0
VMEM_LIMIT = 56 * 1024 * 1024

GQ, HQ = GLA_H * GLA_DK, HG_H * HG_DK
GV, HV, MV = GLA_H * GLA_DV, HG_H * HG_DV, ML_H * ML_DV
MQW = ML_H * ML_DK
C_Q = 0
C_K = C_Q + GQ + HQ
C_V = C_K + GQ + HQ
C_OG = C_V + D_MIX
C_MQ = C_OG + D_MIX
MG = 384
C_MK = C_MQ + MG
N_PROJ = C_MK + MG
GH_W = GQ + HQ
SMALL_OFF = 256
GG_LOC = MQW - SMALL_OFF
GATE_LOC = GG_LOC + GLA_RANK

_SPLITS = (GQ, GQ, GV, GLA_RANK, GV, HQ, HQ, HV, HV, MQW, MQW, MV, ML_H, ML_H, MV)
_NAMES = ("gq", "gk", "gv", "gg", "gr", "hq", "hf", "hi", "hg", "mq", "mk", "mv", "mi", "mf", "mo")
_SRC = {}
_acc = 0
for _n, _s in zip(_NAMES, _SPLITS):
    _SRC[_n] = (_acc, _s)
    _acc += _s
N_IN = _acc


def _proj_perm():
    idx = np.zeros((N_PROJ,), np.int32)
    msk = np.zeros((N_PROJ,), np.float32)

    def put(dst, name):
        s, n = _SRC[name]
        idx[dst:dst + n] = np.arange(s, s + n)
        msk[dst:dst + n] = 1.0

    put(C_Q, "gq"); put(C_Q + GQ, "hq")
    put(C_K, "gk"); put(C_K + GQ, "hf")
    put(C_V, "gv"); put(C_V + GV, "hi"); put(C_V + GV + HV, "mv")
    put(C_OG, "gr"); put(C_OG + GV, "hg"); put(C_OG + GV + HV, "mo")
    put(C_MQ, "mq"); put(C_MQ + MQW, "gg"); put(C_MQ + MQW + GLA_RANK, "mi")
    put(C_MK, "mk"); put(C_MK + MQW + GLA_RANK, "mf")
    return idx, msk


_PERM_IDX, _PERM_MSK = _proj_perm()


def _heads():
    gh = []
    for h in range(GLA_H):
        q0, v0 = h * GLA_DK, h * GLA_DV
        gh.append((q0 // LANES, q0 % LANES, GLA_DK, v0 // LANES, v0 % LANES))
    for h in range(HG_H):
        q0, v0 = GQ + h * HG_DK, GV + h * HG_DV
        gh.append((q0 // LANES, q0 % LANES, HG_DK, v0 // LANES, v0 % LANES))
    ml = []
    for h in range(ML_H):
        q0, v0 = h * ML_DK, GV + HV + h * ML_DV
        ml.append((q0 // LANES, q0 % LANES, ML_DK, v0 // LANES, v0 % LANES))
    return gh, ml


GH_HEADS, ML_HEADS = _heads()
N_GH = len(GH_HEADS)
HEAD_DV = 64


def _sigmoid(x):
    return 1.0 / (1.0 + jnp.exp(-x))


def _log_sigmoid(x):
    return jnp.minimum(x, 0.0) - jnp.log(1.0 + jnp.exp(-jnp.abs(x)))


def _rms(x, g):
    return x * lax.rsqrt(jnp.mean(x * x, axis=-1, keepdims=True) + EPS) * g


def _lane_mask(lo, width, rows=1):
    lane = lax.broadcasted_iota(jnp.int32, (rows, LANES), 1)
    return (lane >= lo) & (lane < lo + width)


def _dot_nt(a, b):
    return lax.dot_general(a, b, (((1,), (1,)), ((), ())), preferred_element_type=F32)


def _dot_tn(a, b, precision=None):
    return lax.dot_general(a, b, (((0,), (0,)), ((), ())), preferred_element_type=F32,
                           precision=precision)


def _mixer_kernel(x_ref, win_ref, wg_ref, wout_ref, seg_ref, vd_ref, v512_ref, cp_ref,
                  sgh_in, cml_in, nml_in, mml_in, cbuf_in,
                  y_ref, sgh_ref, cml_ref, nml_ref, mml_ref, cbuf_ref,
                  p_ref, q_ref, k_ref, la_ref, mq_ref, mk_ref, ig_ref, lf_ref, xp_ref, cat_ref,
                  *, bb, tb, chunk):
    t = pl.program_id(1)
    rows = bb * tb
    nb = chunk // SUB

    @pl.when(t == 0)
    def _():
        sgh_ref[...] = sgh_in[...]
        cml_ref[...] = cml_in[...]
        nml_ref[...] = nml_in[...]
        mml_ref[...] = mml_in[...]
        cbuf_ref[...] = cbuf_in[...]

    x = x_ref[...].reshape(rows, D_MODEL)
    xn = _rms(x, vd_ref[0:1, :]).astype(BF16)
    p_ref[...] = jnp.dot(xn, win_ref[...], preferred_element_type=F32)

    half = GH_W // 2
    small_q = p_ref[:, C_MQ + SMALL_OFF:C_MQ + MG]
    z = jnp.dot(small_q.astype(BF16), wg_ref[...], preferred_element_type=F32) + v512_ref[0:1, 0:half]
    la_gla = _log_sigmoid(z) * (1.0 / GLA_GATE_NORM)
    lane_h = lax.broadcasted_iota(jnp.int32, (1, half), 1)
    is_gla = lane_h < GQ
    for c0 in (0, half):
        qraw = p_ref[:, C_Q + c0:C_Q + c0 + half]
        fr = p_ref[:, C_K + c0:C_K + c0 + half]
        lb = v512_ref[1:2, c0:c0 + half]
        q_h = qraw * _sigmoid(qraw) * (HG_DK ** -0.5)
        la_h = jnp.log(lb + (1.0 - lb) * _sigmoid(fr))
        k_h = (1.0 - lb) * _sigmoid(-fr)
        if c0 == 0:
            q_ref[:, 0:half] = jnp.where(is_gla, qraw * (GLA_DK ** -0.5), q_h)
            k_ref[:, 0:half] = jnp.where(is_gla, fr, k_h)
            la_ref[:, 0:half] = jnp.where(is_gla, la_gla, la_h)
        else:
            q_ref[:, half:] = q_h
            k_ref[:, half:] = k_h
            la_ref[:, half:] = la_h

    pre = SUBLANES
    xp_ref[:, 0:pre, :] = cbuf_ref[...]
    xp_ref[:, pre:pre + tb, :] = p_ref[:, C_MQ:C_MQ + 2 * MG].reshape(bb, tb, 2 * MG)
    conv = cp_ref[ML_CONV:ML_CONV + 1, :].reshape(1, 1, 2 * MG)
    for j in range(ML_CONV):
        off = pre - (ML_CONV - 1) + j
        conv = conv + xp_ref[:, off:off + tb, :] * cp_ref[j:j + 1, :].reshape(1, 1, 2 * MG)
    cbuf_ref[...] = xp_ref[:, tb:tb + pre, :]
    qk = (conv * _sigmoid(conv)).reshape(rows, 2 * MG)
    mq_ref[...] = qk[:, 0:MG]
    mk_ref[...] = qk[:, MG:] * (ML_DK ** -0.5)
    ig_ref[...] = small_q + cp_ref[ML_CONV + 1:ML_CONV + 2, 0:LANES]
    lf_ref[...] = _log_sigmoid(p_ref[:, C_MK + SMALL_OFF:C_MK + MG] + cp_ref[ML_CONV + 1:ML_CONV + 2, LANES:2 * LANES])

    sub_sh = SUB.bit_length() - 1
    chunk_sh = chunk.bit_length() - 1
    assert (1 << sub_sh) == SUB and (1 << chunk_sh) == chunk
    ri = lax.broadcasted_iota(jnp.int32, (chunk, chunk), 0)
    ci = lax.broadcasted_iota(jnp.int32, (chunk, chunk), 1)
    causal = ci <= ri
    tri = causal.astype(F32)
    tri_blk = (causal & ((ri >> sub_sh) == (ci >> sub_sh))).astype(F32)
    eye = (ri == ci).astype(F32)
    cum_mat = jnp.concatenate([tri, tri_blk], axis=0)
    sr = lax.broadcasted_iota(jnp.int32, (chunk, nb * chunk), 0)
    sc = lax.broadcasted_iota(jnp.int32, (chunk, nb * chunk), 1)
    slab, col = sc >> chunk_sh, sc & (chunk - 1)
    off_ok = (slab < nb - 1) & ((sr >> sub_sh) == slab + 1) & (col < (slab + 1) * SUB)
    diag_ok = (slab == nb - 1) & ((sr >> sub_sh) == (col >> sub_sh)) & (col <= sr)
    score_ok = off_ok | diag_ok
    row_id = lax.broadcasted_iota(jnp.int32, (chunk, 1), 0)

    def chunk_body(ic, carry):
        b_i = ic // (tb // chunk)
        r0 = pl.multiple_of(ic * chunk, chunk)
        rs = pl.ds(r0, chunk)

        la = la_ref[rs, :]
        cums = jnp.dot(cum_mat, la, precision=HIGHEST, preferred_element_type=F32)
        b = cums[0:chunk]
        bq = cums[chunk:]
        q = q_ref[rs, :]
        k = k_ref[rs, :]
        b_last = b[chunk - 1:chunk, :]
        q_in = (q * jnp.exp(bq)).astype(BF16)
        q_st = (q * jnp.exp(b)).astype(BF16)
        k_st = (k * jnp.exp(b_last - b)).astype(BF16)
        dec = jnp.exp(b_last)
        pieces = []
        for i in range(1, nb):
            r_i = b[i * SUB - 1:i * SUB, :]
            kt = k * jnp.exp(jnp.minimum(r_i - b, 0.0))
            pieces.append(jnp.where(row_id < i * SUB, kt, 0.0))
        pieces.append(k * jnp.exp(jnp.minimum(-bq, EXP_CLAMP)))
        k_stack = jnp.concatenate(pieces, axis=0).astype(BF16)
        v_all = p_ref[rs, C_V:C_V + D_MIX].astype(BF16)
        v_rep = jnp.concatenate([v_all] * nb, axis=0)

        out_cols = [None] * (D_MIX // LANES)

        def add_out(vv, val, mask):
            val = jnp.where(mask, val, 0.0)
            out_cols[vv] = val if out_cols[vv] is None else out_cols[vv] + val

        for h, (qv, qlo, dk, vv, vlo) in enumerate(GH_HEADS):
            qm = _lane_mask(qlo, dk)
            vm = _lane_mask(vlo, HEAD_DV)
            qs = slice(qv * LANES, (qv + 1) * LANES)
            vs = slice(vv * LANES, (vv + 1) * LANES)
            qh = jnp.where(qm, q_in[:, qs], 0)
            scores = _dot_nt(qh, k_stack[:, qs])
            scores = jnp.where(score_ok, scores, 0.0).astype(BF16)
            o = jnp.dot(scores, v_rep[:, vs], preferred_element_type=F32)
            st = sgh_ref[b_i, h]
            o = o + _dot_nt(jnp.where(qm, q_st[:, qs], 0), st.astype(BF16))
            add_out(vv, o, vm)
            upd = _dot_tn(jnp.where(vm, v_all[:, vs], 0), jnp.where(qm, k_st[:, qs], 0))
            sgh_ref[b_i, h] = st * dec[:, qs] + upd

        lf = lf_ref[rs, :]
        ig = ig_ref[rs, :]
        bm = jnp.dot(tri, lf, precision=HIGHEST, preferred_element_type=F32)
        u_t = _dot_tn(ig - bm, eye, precision=HIGHEST)
        m_all = mml_ref[b_i]
        for h, (qv, qlo, dk, vv, vlo) in enumerate(ML_HEADS):
            qm = _lane_mask(qlo, dk)
            vm = _lane_mask(vlo, HEAD_DV)
            qs = slice(qv * LANES, (qv + 1) * LANES)
            vs = slice(vv * LANES, (vv + 1) * LANES)
            gl = GATE_LOC + h
            bcol = bm[:, gl:gl + 1]
            igcol = ig[:, gl:gl + 1]
            urow = u_t[gl:gl + 1, :]
            m_prev = m_all[:, h:h + 1]
            dlog = jnp.where(causal, bcol + urow, NEG_BIG)
            g = bcol + m_prev
            m_i = jnp.maximum(g, jnp.max(dlog, axis=1, keepdims=True))
            w = jnp.where(causal, jnp.exp(dlog - m_i), 0.0)
            w0 = jnp.exp(g - m_i)
            qf = jnp.where(qm, mq_ref[rs, qs], 0.0)
            kf = jnp.where(qm, mk_ref[rs, qs], 0.0)
            qb = qf.astype(BF16)
            s = _dot_nt(qb, kf.astype(BF16)) * w
            vh = jnp.where(vm, v_all[:, vs], 0)
            c_st = cml_ref[b_i, h]
            num = (jnp.dot(s.astype(BF16), vh, preferred_element_type=F32)
                   + w0 * jnp.dot(qb, c_st.astype(BF16), preferred_element_type=F32))
            n_row = nml_ref[b_i, :, qs]
            den = (jnp.sum(s, axis=1, keepdims=True)
                   + w0 * jnp.sum(qf * n_row, axis=1, keepdims=True))
            hout = num / jnp.maximum(jnp.abs(den), jnp.exp(-m_i))
            add_out(vv, hout, vm)
            m_last = m_i[chunk - 1:chunk, :]
            bl = bcol[chunk - 1:chunk, :]
            wk = jnp.exp(bl - bcol + igcol - m_last)
            dc = jnp.exp(bl + m_prev - m_last)
            kw = kf * wk
            cml_ref[b_i, h] = dc * c_st + _dot_tn(kw.astype(BF16), vh)
            nml_ref[b_i, :, qs] = jnp.where(qm, dc * n_row + jnp.sum(kw, axis=0, keepdims=True), n_row)
            m_all = jnp.where(_lane_mask(h, 1), m_last, m_all)
        mml_ref[b_i] = m_all

        for vv, val in enumerate(out_cols):
            cat_ref[rs, vv * LANES:(vv + 1) * LANES] = val
        return carry

    lax.fori_loop(0, rows // chunk, chunk_body, 0)

    o = cat_ref[...]
    ms = jnp.dot((o * o).astype(BF16), seg_ref[...], preferred_element_type=F32)
    og = p_ref[:, C_OG:C_OG + D_MIX]
    lane_o = lax.broadcasted_iota(jnp.int32, (1, D_MIX), 1)
    gate = _sigmoid(og) * jnp.where(lane_o < GV + HV, og, 1.0)
    cat = (o * lax.rsqrt(ms + EPS) * vd_ref[2:3, :] * gate).astype(BF16)
    hm = jnp.dot(cat, wout_ref[...], preferred_element_type=F32)
    y = x + _rms(hm, vd_ref[1:2, :])
    y_ref[...] = y.reshape(bb, tb, D_MODEL)


def _const_spec(shape):
    nd = len(shape)
    return pl.BlockSpec(shape, lambda b, t: (0,) * nd, pipeline_mode=pl.Buffered(1))


def _state_spec(shape, bb):
    nd = len(shape)
    return pl.BlockSpec((bb,) + tuple(shape[1:]), lambda b, t: (b,) + (0,) * (nd - 1))


def _mixer_call(x, lw, st, *, bb, tb):
    B, T, _ = x.shape
    chunk = min(MAX_CHUNK, tb)
    rows = bb * tb
    grid = (B // bb, T // tb)
    x_spec = pl.BlockSpec((bb, tb, D_MODEL), lambda b, t: (b, t, 0))
    consts = (lw["w_in"], lw["w_gate"], lw["w_out"], lw["seg"], lw["vec_d"], lw["vec_512"], lw["conv_p"])
    states = (st["sgh"], st["cml"], st["nml"], st["mml"], st["cbuf"])
    in_specs = ([x_spec] + [_const_spec(c.shape) for c in consts]
                + [_state_spec(s.shape, bb) for s in states])
    out_shape = ([jax.ShapeDtypeStruct(x.shape, F32)]
                 + [jax.ShapeDtypeStruct(s.shape, F32) for s in states])
    out_specs = [x_spec] + [_state_spec(s.shape, bb) for s in states]
    scratch = [
        pltpu.VMEM((rows, N_PROJ), F32),
        pltpu.VMEM((rows, GH_W), F32),
        pltpu.VMEM((rows, GH_W), F32),
        pltpu.VMEM((rows, GH_W), F32),
        pltpu.VMEM((rows, MG), F32),
        pltpu.VMEM((rows, MG), F32),
        pltpu.VMEM((rows, LANES), F32),
        pltpu.VMEM((rows, LANES), F32),
        pltpu.VMEM((bb, tb + SUBLANES, 2 * MG), F32),
        pltpu.VMEM((rows, D_MIX), F32),
    ]
    outs = pl.pallas_call(
        functools.partial(_mixer_kernel, bb=bb, tb=tb, chunk=chunk),
        grid=grid, in_specs=in_specs, out_specs=out_specs, out_shape=out_shape,
        scratch_shapes=scratch,
        compiler_params=pltpu.CompilerParams(
            dimension_semantics=("arbitrary", "arbitrary"), vmem_limit_bytes=VMEM_LIMIT),
        name="mixer",
    )(x, *consts, *states)
    y = outs[0]
    new = dict(zip(("sgh", "cml", "nml", "mml", "cbuf"), outs[1:]))
    return y, new


def _ffn_kernel(x_ref, wup_ref, wdn_ref, vd_ref, cp_ref, fbuf_in, y_ref, fbuf_ref, xp_ref,
                *, bb, tb):
    t = pl.program_id(1)
    rows = bb * tb
    pre = SUBLANES

    @pl.when(t == 0)
    def _():
        fbuf_ref[...] = fbuf_in[...]

    x = x_ref[...].reshape(rows, D_MODEL)
    xn = _rms(x, vd_ref[0:1, :]).astype(BF16)
    up = jnp.dot(xn, wup_ref[...], preferred_element_type=F32)
    xp_ref[:, 0:pre, :] = fbuf_ref[...]
    xp_ref[:, pre:pre + tb, :] = up.reshape(bb, tb, 2 * D_FF)
    conv = cp_ref[FFN_CONV:FFN_CONV + 1, :].reshape(1, 1, 2 * D_FF)
    for j in range(FFN_CONV):
        off = pre - (FFN_CONV - 1) + j
        conv = conv + xp_ref[:, off:off + tb, :] * cp_ref[j:j + 1, :].reshape(1, 1, 2 * D_FF)
    fbuf_ref[...] = xp_ref[:, tb:tb + pre, :]
    conv = conv.reshape(rows, 2 * D_FF)
    gate = conv[:, 0:D_FF]
    val = conv[:, D_FF:]
    c0 = 0.7978845608028654
    gelu = 0.5 * gate * (1.0 + jnp.tanh(c0 * (gate + 0.044715 * gate * gate * gate)))
    hmid = (gelu * val).astype(BF16)
    hd = jnp.dot(hmid, wdn_ref[...], preferred_element_type=F32)
    y = x + _rms(hd, vd_ref[1:2, :])
    y_ref[...] = y.reshape(bb, tb, D_MODEL)


def _ffn_call(x, lw, fbuf, *, bb, tb):
    B, T, _ = x.shape
    grid = (B // bb, T // tb)
    x_spec = pl.BlockSpec((bb, tb, D_MODEL), lambda b, t: (b, t, 0))
    consts = (lw["w_up"], lw["w_down"], lw["vec_ffn"], lw["ffn_conv_p"])
    in_specs = [x_spec] + [_const_spec(c.shape) for c in consts] + [_state_spec(fbuf.shape, bb)]
    out_shape = [jax.ShapeDtypeStruct(x.shape, F32), jax.ShapeDtypeStruct(fbuf.shape, F32)]
    out_specs = [x_spec, _state_spec(fbuf.shape, bb)]
    y, fnew = pl.pallas_call(
        functools.partial(_ffn_kernel, bb=bb, tb=tb),
        grid=grid, in_specs=in_specs, out_specs=out_specs, out_shape=out_shape,
        scratch_shapes=[pltpu.VMEM((bb, tb + SUBLANES, 2 * D_FF), F32)],
        compiler_params=pltpu.CompilerParams(
            dimension_semantics=("arbitrary", "arbitrary"), vmem_limit_bytes=VMEM_LIMIT),
        name="ffn",
    )(x, *consts, fbuf)
    return y, fnew


def _pad_rows(a, n=SUBLANES):
    return jnp.pad(a, ((0, n - a.shape[0]), (0, 0)))


def _layer_weights(l, lb_all, g_mix_pre, g_mix_post, g_ffn_pre, g_ffn_post, w_in, gla_w_gate,
                   gla_b_gate, ml_conv_w, ml_conv_b, ml_b_i, ml_b_f, g_head, w_out, ffn_w_up,
                   ffn_conv_w, ffn_conv_b, ffn_w_down):
    w_in_p = (jnp.take(w_in[l], jnp.asarray(_PERM_IDX), axis=1) * jnp.asarray(_PERM_MSK)).astype(BF16)
    half = GH_W // 2
    w_gate = jnp.zeros((LANES, half), F32).at[GG_LOC:GG_LOC + GLA_RANK, 0:GQ].set(gla_w_gate[l]).astype(BF16)
    head_id = np.arange(D_MIX) // HEAD_DV
    seg = jnp.asarray((head_id[:, None] == head_id[None, :]).astype(np.float32) / HEAD_DV, dtype=BF16)
    vec_d = _pad_rows(jnp.stack([g_mix_pre[l], g_mix_post[l], g_head[l]]))
    vec_512 = _pad_rows(jnp.stack([jnp.pad(gla_b_gate[l], (0, HQ)), jnp.pad(lb_all[l], (GQ, 0))]))

    def mqk(a):
        z = jnp.zeros(a.shape[:-1] + (MG - MQW,), F32)
        return jnp.concatenate([a[..., :MQW], z, a[..., MQW:], z], axis=-1)

    gate_b = jnp.zeros((2 * LANES,), F32)
    gate_b = gate_b.at[GATE_LOC:GATE_LOC + ML_H].set(ml_b_i[l])
    gate_b = gate_b.at[LANES + GATE_LOC:LANES + GATE_LOC + ML_H].set(ml_b_f[l])
    conv_p = _pad_rows(jnp.concatenate(
        [mqk(ml_conv_w[l]), mqk(ml_conv_b[l])[None], jnp.pad(gate_b, (0, 2 * MG - 2 * LANES))[None]], axis=0))
    return dict(
        w_in=w_in_p, w_gate=w_gate, w_out=w_out[l].astype(BF16), seg=seg, vec_d=vec_d,
        vec_512=vec_512, conv_p=conv_p,
        w_up=ffn_w_up[l].astype(BF16), w_down=ffn_w_down[l].astype(BF16),
        vec_ffn=_pad_rows(jnp.stack([g_ffn_pre[l], g_ffn_post[l]])),
        ffn_conv_p=_pad_rows(jnp.concatenate([ffn_conv_w[l], ffn_conv_b[l][None]], axis=0)),
    )


def _pack_states(s_gla, s_hg, c_ml, n_ml, m_ml, buf_ml, buf_ffn):
    B = s_gla.shape[0]
    blocks = []
    for h, (qv, qlo, dk, vv, vlo) in enumerate(GH_HEADS):
        s = s_gla[:, h] if h < GLA_H else s_hg[:, h - GLA_H]
        s_t = jnp.swapaxes(s, 1, 2)
        blocks.append(jnp.pad(s_t, ((0, 0), (vlo, LANES - vlo - HEAD_DV), (qlo, LANES - qlo - dk))))
    sgh = jnp.stack(blocks, axis=1)
    blocks = []
    for h, (qv, qlo, dk, vv, vlo) in enumerate(ML_HEADS):
        blocks.append(jnp.pad(c_ml[:, h], ((0, 0), (qlo, LANES - qlo - dk), (vlo, LANES - vlo - HEAD_DV))))
    cml = jnp.stack(blocks, axis=1)
    nml = jnp.pad(n_ml.reshape(B, 1, MQW), ((0, 0), (0, 0), (0, MG - MQW)))
    mml = jnp.pad(m_ml.reshape(B, 1, ML_H), ((0, 0), (0, 0), (0, LANES - ML_H)))
    z = jnp.zeros((B, ML_CONV - 1, MG - MQW), F32)
    cb = jnp.concatenate([buf_ml[..., :MQW], z, buf_ml[..., MQW:], z], axis=-1)
    cbuf = jnp.pad(cb, ((0, 0), (SUBLANES - (ML_CONV - 1), 0), (0, 0)))
    fbuf = jnp.pad(buf_ffn, ((0, 0), (SUBLANES - (FFN_CONV - 1), 0), (0, 0)))
    return dict(sgh=sgh, cml=cml, nml=nml, mml=mml, cbuf=cbuf), fbuf


def _unpack_states(st, fbuf):
    sgh, cml = st["sgh"], st["cml"]
    gl, hg, cm = [], [], []
    for h, (qv, qlo, dk, vv, vlo) in enumerate(GH_HEADS):
        s = jnp.swapaxes(sgh[:, h, vlo:vlo + HEAD_DV, qlo:qlo + dk], 1, 2)
        (gl if h < GLA_H else hg).append(s)
    for h, (qv, qlo, dk, vv, vlo) in enumerate(ML_HEADS):
        cm.append(cml[:, h, qlo:qlo + dk, vlo:vlo + HEAD_DV])
    B = sgh.shape[0]
    n_ml = st["nml"][:, 0, :MQW].reshape(B, ML_H, ML_DK)
    m_ml = st["mml"][:, 0, :ML_H]
    cb = st["cbuf"][:, SUBLANES - (ML_CONV - 1):, :]
    buf_ml = jnp.concatenate([cb[..., :MQW], cb[..., MG:MG + MQW]], axis=-1)
    buf_ffn = fbuf[:, SUBLANES - (FFN_CONV - 1):, :]
    return (jnp.stack(gl, 1), jnp.stack(hg, 1), jnp.stack(cm, 1), n_ml, m_ml, buf_ml, buf_ffn)


def _block_sizes(B, T):
    if T >= 256:
        return 1, 256
    bb = 4 if B % 4 == 0 else 1
    return bb, T


def _run_trunk(x, states, weights):
    B, T, _ = x.shape
    bb, tb = _block_sizes(B, T)
    new = [[] for _ in range(7)]
    for l in range(DEPTH):
        st, fbuf = _pack_states(*(s[l] for s in states))
        x, st = _mixer_call(x, weights[l], st, bb=bb, tb=tb)
        x, fbuf = _ffn_call(x, weights[l], fbuf, bb=bb, tb=tb)
        for i, s in enumerate(_unpack_states(st, fbuf)):
            new[i].append(s)
    return x, [jnp.stack(s, axis=0) for s in new]


def kernel(x_prompt, x_sample, state_gla, state_hgrn, state_mlstm_C, state_mlstm_n,
           state_mlstm_m, cache_mlstm_conv, cache_ffn_conv, g_mix_pre, g_mix_post, g_ffn_pre,
           g_ffn_post, w_in, gla_w_gate, gla_b_gate, hgrn_lb, ml_conv_w, ml_conv_b, ml_b_i,
           ml_b_f, g_head, w_out, ffn_w_up, ffn_conv_w, ffn_conv_b, ffn_w_down):
    sm = jax.nn.softmax(hgrn_lb.astype(F32), axis=0)
    lb_all = jnp.cumsum(sm, axis=0) - sm[0:1]
    weights = [
        _layer_weights(l, lb_all, g_mix_pre, g_mix_post, g_ffn_pre, g_ffn_post, w_in, gla_w_gate,
                       gla_b_gate, ml_conv_w, ml_conv_b, ml_b_i, ml_b_f, g_head, w_out, ffn_w_up,
                       ffn_conv_w, ffn_conv_b, ffn_w_down)
        for l in range(DEPTH)]
    bp = x_prompt.shape[0]
    zero_states = (
        jnp.zeros((DEPTH, bp, GLA_H, GLA_DK, GLA_DV), F32),
        jnp.zeros((DEPTH, bp, HG_H, HG_DK, HG_DV), F32),
        jnp.zeros((DEPTH, bp, ML_H, ML_DK, ML_DV), F32),
        jnp.zeros((DEPTH, bp, ML_H, ML_DK), F32),
        jnp.zeros((DEPTH, bp, ML_H), F32),
        jnp.zeros((DEPTH, bp, ML_CONV - 1, 2 * ML_H * ML_DK), F32),
        jnp.zeros((DEPTH, bp, FFN_CONV - 1, 2 * D_FF), F32),
    )
    y_prompt, ps = _run_trunk(x_prompt, zero_states, weights)
    sample_states = (state_gla, state_hgrn, state_mlstm_C, state_mlstm_n, state_mlstm_m,
                     cache_mlstm_conv, cache_ffn_conv)
    y_sample, ss = _run_trunk(x_sample, sample_states, weights)
    return (y_prompt, y_sample, *ps, *ss)
```

```python
import functools

import numpy as np
import jax
import jax.numpy as jnp
from jax import lax
from jax.experimental import pallas as pl
from jax.experimental.pallas import tpu as pltpu

F32 = jnp.float32
BF16 = jnp.bfloat16
HIGHEST = lax.Precision.HIGHEST

D_MODEL = 1024
DEPTH = 2
GLA_H, GLA_DK, GLA_DV, GLA_RANK, GLA_GATE_NORM = 6, 32, 64, 16, 16.0
HG_H, HG_DK, HG_DV = 5, 64, 64
ML_H, ML_DK, ML_DV, ML_CONV = 5, 64, 64, 4
D_MIX = GLA_H * GLA_DV + HG_H * HG_DV + ML_H * ML_DV
D_FF = 2816
FFN_CONV = 3
EPS = 1e-6
NEG_BIG = -1e30

LANES = 128
SUBLANES = 8
TILE = 256
SLOT = 64
SUB = 16
MAX_CHUNK = 64
EXP_CLAMP = 80.0
VMEM_LIMIT = 56 * 1024 * 1024

GQ, HQ = GLA_H * GLA_DK, HG_H * HG_DK
GV, HV, MV = GLA_H * GLA_DV, HG_H * HG_DV, ML_H * ML_DV
MQW = ML_H * ML_DK
GH_W = GQ + HQ
MG = 384
C_Q = 0
C_K = C_Q + GH_W
C_V = C_K + GH_W
C_OG = C_V + D_MIX
C_MQ = C_OG + D_MIX
C_MK = C_MQ + MG
N_PROJ = C_MK + MG
SMALL_OFF = 256
GG_LOC = MQW - SMALL_OFF
GATE_LOC = GG_LOC + GLA_RANK
ML_ORDER = (1, 2, 3, 4, 0)

_SPLITS = (GQ, GQ, GV, GLA_RANK, GV, HQ, HQ, HV, HV, MQW, MQW, MV, ML_H, ML_H, MV)
_NAMES = ("gq", "gk", "gv", "gg", "gr", "hq", "hf", "hi", "hg", "mq", "mk", "mv", "mi", "mf", "mo")
_SRC = {}
_acc = 0
for _n, _s in zip(_NAMES, _SPLITS):
    _SRC[_n] = _acc
    _acc += _s
N_IN = _acc

SLOT_HEADS = (("g", 0), ("g", 1), ("g", 2), ("g", 3),
              ("g", 4), ("g", 5), ("h", 0), ("m", 0),
              ("h", 1), ("h", 2), ("h", 3), ("h", 4),
              ("m", 1), ("m", 2), ("m", 3), ("m", 4))
_CAT_OFF = {"g": 0, "h": GV, "m": GV + HV}
CAT_PERM = np.concatenate([_CAT_OFF[k] + SLOT * h + np.arange(SLOT) for k, h in SLOT_HEADS])
ML_LANES = np.concatenate([np.full(SLOT, k == "m") for k, _ in SLOT_HEADS])

GH_TILES = (
    dict(q0=0, qw=128, vt=0, heads=tuple((32 * r, GLA_DK, r, "g", r) for r in range(4))),
    dict(q0=128, qw=128, vt=1, heads=((0, GLA_DK, 0, "g", 4), (32, GLA_DK, 1, "g", 5), (64, HG_DK, 2, "h", 0))),
    dict(q0=256, qw=256, vt=2, heads=tuple((64 * r, HG_DK, r, "h", r + 1) for r in range(4))),
)
ML_GROUPS = (
    dict(q0=0, qw=256, vt=3, heads=tuple((64 * r, r, r + 1) for r in range(4))),
    dict(q0=256, qw=128, vt=1, heads=((0, 3, 0),)),
)


def _proj_segments():
    segs = [(_SRC["gq"], GQ), (_SRC["hq"], HQ), (_SRC["gk"], GQ), (_SRC["hf"], HQ)]
    for names in (("gv", "hi", "mv"), ("gr", "hg", "mo")):
        src = dict(zip("ghm", names))
        for kind, h in SLOT_HEADS:
            segs.append((_SRC[src[kind]] + SLOT * h, SLOT))
    pad = MG - MQW - GLA_RANK - ML_H
    for name, gate in (("mq", "mi"), ("mk", "mf")):
        for h in ML_ORDER:
            segs.append((_SRC[name] + ML_DK * h, ML_DK))
        segs.append((_SRC["gg"], GLA_RANK) if name == "mq" else (None, GLA_RANK))
        segs.append((_SRC[gate], ML_H))
        segs.append((None, pad))
    merged = []
    for s, n in segs:
        if merged and s is not None and merged[-1][0] is not None and merged[-1][0] + merged[-1][1] == s:
            merged[-1] = (merged[-1][0], merged[-1][1] + n)
        else:
            merged.append((s, n))
    assert sum(n for _, n in merged) == N_PROJ
    return merged


_PROJ_SEGS = _proj_segments()


def _sigmoid(x):
    return 1.0 / (1.0 + jnp.exp(-x))


def _log_sigmoid(x):
    return jnp.minimum(x, 0.0) - jnp.log(1.0 + jnp.exp(-jnp.abs(x)))


def _rms(x, g):
    return x * lax.rsqrt(jnp.mean(x * x, axis=-1, keepdims=True) + EPS) * g


def _split3(x):
    hi = x.astype(BF16)
    r1 = x - hi.astype(F32)
    mid = r1.astype(BF16)
    lo = (r1 - mid.astype(F32)).astype(BF16)
    return hi, mid, lo


def _in_range(idx, lo, width):
    return (idx >= lo) & (idx < lo + width)


def _dot(a, b, precision=None):
    return jnp.dot(a, b, preferred_element_type=F32, precision=precision)


def _dot_nt(a, b):
    return lax.dot_general(a, b, (((1,), (1,)), ((), ())), preferred_element_type=F32)


def _dot_tn(a, b, precision=None):
    return lax.dot_general(a, b, (((0,), (0,)), ((), ())), preferred_element_type=F32,
                           precision=precision)


def _mixer_kernel(x_ref, win_ref, wg_ref, wout_ref, seg_ref, vd_ref, v512_ref, cp_ref,
                  s0_in, s1_in, s2_in, ca_in, cb_in, nml_in, mml_in, cbuf_in,
                  y_ref, s0_ref, s1_ref, s2_ref, ca_ref, cb_ref, nml_ref, mml_ref, cbuf_ref,
                  p_ref, q_ref, k_ref, la_ref, mq_ref, mk_ref, ig_ref, lf_ref, xp_ref, cat_ref,
                  *, bb, tb, chunk):
    t = pl.program_id(1)
    rows = bb * tb
    nb = chunk // SUB
    state_pairs = ((s0_in, s0_ref), (s1_in, s1_ref), (s2_in, s2_ref), (ca_in, ca_ref),
                   (cb_in, cb_ref), (nml_in, nml_ref), (mml_in, mml_ref), (cbuf_in, cbuf_ref))

    @pl.when(t == 0)
    def _():
        for src, dst in state_pairs:
            dst[...] = src[...]

    x = x_ref[...].reshape(rows, D_MODEL)
    xn = _rms(x, vd_ref[0:1, :]).astype(BF16)
    p_ref[...] = _dot(xn, win_ref[...])

    half = GH_W // 2
    small_q = p_ref[:, C_MQ + SMALL_OFF:C_MQ + MG]
    z = _dot(small_q.astype(BF16), wg_ref[...]) + v512_ref[0:1, 0:half]
    la_gla = _log_sigmoid(z) * (1.0 / GLA_GATE_NORM)
    is_gla = lax.broadcasted_iota(jnp.int32, (1, half), 1) < GQ
    for c0 in (0, half):
        qraw = p_ref[:, C_Q + c0:C_Q + c0 + half]
        fr = p_ref[:, C_K + c0:C_K + c0 + half]
        lb = v512_ref[1:2, c0:c0 + half]
        q_h = qraw * _sigmoid(qraw) * (HG_DK ** -0.5)
        la_h = jnp.log(lb + (1.0 - lb) * _sigmoid(fr))
        k_h = (1.0 - lb) * _sigmoid(-fr)
        if c0 == 0:
            q_ref[:, 0:half] = jnp.where(is_gla, qraw * (GLA_DK ** -0.5), q_h)
            k_ref[:, 0:half] = jnp.where(is_gla, fr, k_h)
            la_ref[:, 0:half] = jnp.where(is_gla, la_gla, la_h)
        else:
            q_ref[:, half:] = q_h
            k_ref[:, half:] = k_h
            la_ref[:, half:] = la_h

    pre = SUBLANES
    xp_ref[:, 0:pre, :] = cbuf_ref[...]
    xp_ref[:, pre:pre + tb, :] = p_ref[:, C_MQ:C_MQ + 2 * MG].reshape(bb, tb, 2 * MG)
    conv = cp_ref[ML_CONV:ML_CONV + 1, :].reshape(1, 1, 2 * MG)
    for j in range(ML_CONV):
        off = pre - (ML_CONV - 1) + j
        conv = conv + xp_ref[:, off:off + tb, :] * cp_ref[j:j + 1, :].reshape(1, 1, 2 * MG)
    cbuf_ref[...] = xp_ref[:, tb:tb + pre, :]
    qk = (conv * _sigmoid(conv)).reshape(rows, 2 * MG)
    mq_ref[...] = qk[:, 0:MG]
    mk_ref[...] = qk[:, MG:] * (ML_DK ** -0.5)
    ig_ref[...] = small_q + cp_ref[ML_CONV + 1:ML_CONV + 2, 0:LANES]
    lf_ref[...] = _log_sigmoid(p_ref[:, C_MK + SMALL_OFF:C_MK + MG] + cp_ref[ML_CONV + 1:ML_CONV + 2, LANES:2 * LANES])

    sub_sh = SUB.bit_length() - 1
    chunk_sh = chunk.bit_length() - 1
    assert (1 << sub_sh) == SUB and (1 << chunk_sh) == chunk
    ri = lax.broadcasted_iota(jnp.int32, (chunk, chunk), 0)
    ci = lax.broadcasted_iota(jnp.int32, (chunk, chunk), 1)
    causal = ci <= ri
    tri = causal.astype(BF16)
    tri_blk = (causal & ((ri >> sub_sh) == (ci >> sub_sh))).astype(BF16)
    tri3 = jnp.concatenate([tri] * 3, axis=1)
    cum3 = jnp.concatenate([tri3, jnp.concatenate([tri_blk] * 3, axis=1)], axis=0)
    li = lax.broadcasted_iota(jnp.int32, (LANES, 3 * LANES), 0)
    lj = lax.broadcasted_iota(jnp.int32, (LANES, 3 * LANES), 1) & (LANES - 1)
    eye3 = (li == lj).astype(BF16)
    max_heads = TILE // SLOT
    sr = lax.broadcasted_iota(jnp.int32, (max_heads * chunk, nb * chunk), 0) & (chunk - 1)
    sc = lax.broadcasted_iota(jnp.int32, (max_heads * chunk, nb * chunk), 1)
    slab, col = sc >> chunk_sh, sc & (chunk - 1)
    off_ok = (slab < nb - 1) & ((sr >> sub_sh) == slab + 1) & (col < (slab + 1) * SUB)
    diag_ok = (slab == nb - 1) & ((sr >> sub_sh) == (col >> sub_sh)) & (col <= sr)
    score_ok = off_ok | diag_ok
    row_id = lax.broadcasted_iota(jnp.int32, (chunk, 1), 0)
    lane_v = lax.broadcasted_iota(jnp.int32, (1, TILE), 1)
    lane_m = lax.broadcasted_iota(jnp.int32, (1, LANES), 1)
    gh_refs = (s0_ref, s1_ref, s2_ref)
    ml_refs = (ca_ref, cb_ref)
    chunks_per_stream = tb // chunk

    def chunk_body(ic, carry):
        b_i = ic // chunks_per_stream
        rs = pl.ds(pl.multiple_of(ic * chunk, chunk), chunk)
        v_all = p_ref[rs, C_V:C_V + D_MIX].astype(BF16)
        out_tiles = [None] * (D_MIX // TILE)

        def add_out(vt, val):
            out_tiles[vt] = val if out_tiles[vt] is None else out_tiles[vt] + val

        la = la_ref[rs, :]
        cums = _dot(cum3, jnp.concatenate(_split3(la), axis=0))
        b = cums[0:chunk]
        bq = cums[chunk:]
        q = q_ref[rs, :]
        k = k_ref[rs, :]
        b_last = b[chunk - 1:chunk, :]
        q_in = (q * jnp.exp(bq)).astype(BF16)
        q_st = (q * jnp.exp(b)).astype(BF16)
        k_st = (k * jnp.exp(b_last - b)).astype(BF16)
        dec = jnp.exp(b_last)
        pieces = []
        for i in range(1, nb):
            r_i = b[i * SUB - 1:i * SUB, :]
            kt = k * jnp.exp(jnp.minimum(r_i - b, 0.0))
            pieces.append(jnp.where(row_id < i * SUB, kt, 0.0))
        pieces.append(k * jnp.exp(jnp.minimum(-bq, EXP_CLAMP)))
        k_stack = jnp.concatenate(pieces, axis=0).astype(BF16)
        v_rep = jnp.concatenate([v_all] * nb, axis=0)

        for tile, s_ref in zip(GH_TILES, gh_refs):
            q0, qw, vt, heads = tile["q0"], tile["qw"], tile["vt"], tile["heads"]
            nr = len(heads)
            qs = slice(q0, q0 + qw)
            vs = slice(vt * TILE, (vt + 1) * TILE)
            lane_q = lax.broadcasted_iota(jnp.int32, (1, qw), 1)
            row_s = lax.broadcasted_iota(jnp.int32, (TILE, 1), 0)
            qmasks = [_in_range(lane_q, qoff, dk) for qoff, dk, _, _, _ in heads]
            lhs_in = jnp.concatenate([jnp.where(m, q_in[:, qs], 0) for m in qmasks], axis=0)
            lhs_st = jnp.concatenate([jnp.where(m, q_st[:, qs], 0) for m in qmasks], axis=0)
            scores = _dot_nt(lhs_in, k_stack[:, qs])
            scores = jnp.where(score_ok[0:nr * chunk], scores, 0.0).astype(BF16)
            s_t = s_ref[b_i]
            pv = _dot(scores, v_rep[:, vs]) + _dot_nt(lhs_st, s_t.astype(BF16))
            block = None
            for r, (qoff, dk, slot, _, _) in enumerate(heads):
                add_out(vt, jnp.where(_in_range(lane_v, slot * SLOT, SLOT), pv[r * chunk:(r + 1) * chunk], 0.0))
                m = _in_range(row_s, slot * SLOT, SLOT) & qmasks[r]
                block = m if block is None else block | m
            upd = _dot_tn(v_all[:, vs], k_st[:, qs])
            s_ref[b_i] = s_t * dec[:, qs] + jnp.where(block, upd, 0.0)

        lf = lf_ref[rs, :]
        ig = ig_ref[rs, :]
        bm = _dot(tri3, jnp.concatenate(_split3(lf), axis=0))
        u_t = _dot_nt(eye3, jnp.concatenate(_split3(ig - bm), axis=1))
        m_all = mml_ref[b_i]
        m_new = m_all
        for grp, c_ref in zip(ML_GROUPS, ml_refs):
            q0, qw, vt, heads = grp["q0"], grp["qw"], grp["vt"], grp["heads"]
            qs = slice(q0, q0 + qw)
            vs = slice(vt * TILE, (vt + 1) * TILE)
            lane_q = lax.broadcasted_iota(jnp.int32, (1, qw), 1)
            row_q = lax.broadcasted_iota(jnp.int32, (qw, 1), 0)
            mq = mq_ref[rs, qs]
            mk = mk_ref[rs, qs]
            n_row = nml_ref[b_i, :, qs]
            lhs, w_st, w0_st, mi_st = [], [], [], []
            wk_l = jnp.zeros((chunk, qw), F32)
            dc_q = jnp.zeros((1, qw), F32)
            dc_v = jnp.zeros((1, TILE), F32)
            block = None
            for qoff, slot, h in heads:
                qm = _in_range(lane_q, qoff, ML_DK)
                vm = _in_range(lane_v, slot * SLOT, SLOT)
                gl = GATE_LOC + h
                bcol = bm[:, gl:gl + 1]
                m_prev = m_all[:, h:h + 1]
                dlog = jnp.where(causal, bcol + u_t[gl:gl + 1, :], NEG_BIG)
                g = bcol + m_prev
                m_i = jnp.maximum(g, jnp.max(dlog, axis=1, keepdims=True))
                w_st.append(jnp.where(causal, jnp.exp(dlog - m_i), 0.0))
                w0_st.append(jnp.exp(g - m_i))
                mi_st.append(m_i)
                lhs.append(jnp.where(qm, mq, 0.0))
                m_last = m_i[chunk - 1:chunk, :]
                bl = bcol[chunk - 1:chunk, :]
                wk = jnp.exp(bl - bcol + ig[:, gl:gl + 1] - m_last)
                dc = jnp.exp(bl + m_prev - m_last)
                wk_l = jnp.where(qm, wk, wk_l)
                dc_q = jnp.where(qm, dc, dc_q)
                dc_v = jnp.where(vm, dc, dc_v)
                m_new = jnp.where(lane_m == h, m_last, m_new)
                bm_ = _in_range(row_q, qoff, ML_DK) & vm
                block = bm_ if block is None else block | bm_
            lhs = jnp.concatenate(lhs, axis=0)
            w_st = jnp.concatenate(w_st, axis=0)
            w0_st = jnp.concatenate(w0_st, axis=0)
            mi_st = jnp.concatenate(mi_st, axis=0)
            lhs_b = lhs.astype(BF16)
            s = _dot_nt(lhs_b, mk.astype(BF16)) * w_st
            c_st = c_ref[b_i]
            num = _dot(s.astype(BF16), v_all[:, vs]) + w0_st * _dot(lhs_b, c_st.astype(BF16))
            den = (jnp.sum(s, axis=1, keepdims=True)
                   + w0_st * jnp.sum(lhs * n_row, axis=1, keepdims=True))
            hout = num / jnp.maximum(jnp.abs(den), jnp.exp(-mi_st))
            for r, (qoff, slot, h) in enumerate(heads):
                add_out(vt, jnp.where(_in_range(lane_v, slot * SLOT, SLOT), hout[r * chunk:(r + 1) * chunk], 0.0))
            kw = mk * wk_l
            c_ref[b_i] = c_st * dc_v + jnp.where(block, _dot_tn(kw.astype(BF16), v_all[:, vs]), 0.0)
            nml_ref[b_i, :, qs] = n_row * dc_q + jnp.sum(kw, axis=0, keepdims=True)
        mml_ref[b_i] = m_new

        for vt, val in enumerate(out_tiles):
            cat_ref[rs, vt * TILE:(vt + 1) * TILE] = val
        return carry

    lax.fori_loop(0, rows // chunk, chunk_body, 0)

    o = cat_ref[...]
    ms = _dot((o * o).astype(BF16), seg_ref[...])
    og = p_ref[:, C_OG:C_OG + D_MIX]
    gate = _sigmoid(og) * jnp.where(vd_ref[3:4, :] > 0.5, 1.0, og)
    cat = (o * lax.rsqrt(ms + EPS) * vd_ref[2:3, :] * gate).astype(BF16)
    hm = _dot(cat, wout_ref[...])
    y = x + _rms(hm, vd_ref[1:2, :])
    y_ref[...] = y.reshape(bb, tb, D_MODEL)


def _const_spec(shape):
    nd = len(shape)
    return pl.BlockSpec(shape, lambda b, t: (0,) * nd, pipeline_mode=pl.Buffered(1))


def _state_spec(shape, bb):
    nd = len(shape)
    return pl.BlockSpec((bb,) + tuple(shape[1:]), lambda b, t: (b,) + (0,) * (nd - 1))


_MIXER_STATES = ("s0", "s1", "s2", "ca", "cb", "nml", "mml", "cbuf")


def _mixer_call(x, lw, st, *, bb, tb):
    B, T, _ = x.shape
    chunk = min(MAX_CHUNK, tb)
    rows = bb * tb
    grid = (B // bb, T // tb)
    x_spec = pl.BlockSpec((bb, tb, D_MODEL), lambda b, t: (b, t, 0))
    consts = (lw["w_in"], lw["w_gate"], lw["w_out"], lw["seg"], lw["vec_d"], lw["vec_512"], lw["conv_p"])
    states = tuple(st[n] for n in _MIXER_STATES)
    in_specs = ([x_spec] + [_const_spec(c.shape) for c in consts]
                + [_state_spec(s.shape, bb) for s in states])
    out_shape = ([jax.ShapeDtypeStruct(x.shape, F32)]
                 + [jax.ShapeDtypeStruct(s.shape, F32) for s in states])
    out_specs = [x_spec] + [_state_spec(s.shape, bb) for s in states]
    scratch = [
        pltpu.VMEM((rows, N_PROJ), F32),
        pltpu.VMEM((rows, GH_W), F32),
        pltpu.VMEM((rows, GH_W), F32),
        pltpu.VMEM((rows, GH_W), F32),
        pltpu.VMEM((rows, MG), F32),
        pltpu.VMEM((rows, MG), F32),
        pltpu.VMEM((rows, LANES), F32),
        pltpu.VMEM((rows, LANES), F32),
        pltpu.VMEM((bb, tb + SUBLANES, 2 * MG), F32),
        pltpu.VMEM((rows, D_MIX), F32),
    ]
    outs = pl.pallas_call(
        functools.partial(_mixer_kernel, bb=bb, tb=tb, chunk=chunk),
        grid=grid, in_specs=in_specs, out_specs=out_specs, out_shape=out_shape,
        scratch_shapes=scratch,
        compiler_params=pltpu.CompilerParams(
            dimension_semantics=("arbitrary", "arbitrary"), vmem_limit_bytes=VMEM_LIMIT),
        name="mixer",
    )(x, *consts, *states)
    return outs[0], dict(zip(_MIXER_STATES, outs[1:]))


def _ffn_kernel(x_ref, wup_ref, wdn_ref, vd_ref, cp_ref, fbuf_in, y_ref, fbuf_ref, xp_ref,
                *, bb, tb):
    t = pl.program_id(1)
    rows = bb * tb
    pre = SUBLANES

    @pl.when(t == 0)
    def _():
        fbuf_ref[...] = fbuf_in[...]

    x = x_ref[...].reshape(rows, D_MODEL)
    xn = _rms(x, vd_ref[0:1, :]).astype(BF16)
    up = _dot(xn, wup_ref[...])
    xp_ref[:, 0:pre, :] = fbuf_ref[...]
    xp_ref[:, pre:pre + tb, :] = up.reshape(bb, tb, 2 * D_FF)
    conv = cp_ref[FFN_CONV:FFN_CONV + 1, :].reshape(1, 1, 2 * D_FF)
    for j in range(FFN_CONV):
        off = pre - (FFN_CONV - 1) + j
        conv = conv + xp_ref[:, off:off + tb, :] * cp_ref[j:j + 1, :].reshape(1, 1, 2 * D_FF)
    fbuf_ref[...] = xp_ref[:, tb:tb + pre, :]
    conv = conv.reshape(rows, 2 * D_FF)
    gate = conv[:, 0:D_FF]
    val = conv[:, D_FF:]
    c0 = 0.7978845608028654
    gelu = 0.5 * gate * (1.0 + jnp.tanh(c0 * (gate + 0.044715 * gate * gate * gate)))
    hmid = (gelu * val).astype(BF16)
    hd = _dot(hmid, wdn_ref[...])
    y = x + _rms(hd, vd_ref[1:2, :])
    y_ref[...] = y.reshape(bb, tb, D_MODEL)


def _ffn_call(x, lw, fbuf, *, bb, tb):
    B, T, _ = x.shape
    grid = (B // bb, T // tb)
    x_spec = pl.BlockSpec((bb, tb, D_MODEL), lambda b, t: (b, t, 0))
    consts = (lw["w_up"], lw["w_down"], lw["vec_ffn"], lw["ffn_conv_p"])
    in_specs = [x_spec] + [_const_spec(c.shape) for c in consts] + [_state_spec(fbuf.shape, bb)]
    out_shape = [jax.ShapeDtypeStruct(x.shape, F32), jax.ShapeDtypeStruct(fbuf.shape, F32)]
    out_specs = [x_spec, _state_spec(fbuf.shape, bb)]
    y, fnew = pl.pallas_call(
        functools.partial(_ffn_kernel, bb=bb, tb=tb),
        grid=grid, in_specs=in_specs, out_specs=out_specs, out_shape=out_shape,
        scratch_shapes=[pltpu.VMEM((bb, tb + SUBLANES, 2 * D_FF), F32)],
        compiler_params=pltpu.CompilerParams(
            dimension_semantics=("arbitrary", "arbitrary"), vmem_limit_bytes=VMEM_LIMIT),
        name="ffn",
    )(x, *consts, fbuf)
    return y, fnew


def _pad_rows(a, n=SUBLANES):
    return jnp.pad(a, ((0, n - a.shape[0]), (0, 0)))


def _mqk(a):
    lead = a.shape[:-1]
    parts = []
    for half in (a[..., :MQW], a[..., MQW:]):
        hh = half.reshape(lead + (ML_H, ML_DK))
        parts += [hh[..., h, :] for h in ML_ORDER] + [jnp.zeros(lead + (MG - MQW,), a.dtype)]
    return jnp.concatenate(parts, axis=-1)


def _mqk_inv(a):
    inv = np.argsort(np.asarray(ML_ORDER))
    parts = []
    for base in (0, MG):
        parts += [a[..., base + ML_DK * int(p):base + ML_DK * (int(p) + 1)] for p in inv]
    return jnp.concatenate(parts, axis=-1)


def _to_slots(a, axis):
    parts = [lax.slice_in_dim(a, _CAT_OFF[k] + SLOT * h, _CAT_OFF[k] + SLOT * (h + 1), axis=axis)
             for k, h in SLOT_HEADS]
    return jnp.concatenate(parts, axis=axis)


def _layer_weights(l, lb_all, g_mix_pre, g_mix_post, g_ffn_pre, g_ffn_post, w_in, gla_w_gate,
                   gla_b_gate, ml_conv_w, ml_conv_b, ml_b_i, ml_b_f, g_head, w_out, ffn_w_up,
                   ffn_conv_w, ffn_conv_b, ffn_w_down):
    w = w_in[l]
    cols = [w[:, s:s + n] if s is not None else jnp.zeros((D_MODEL, n), F32) for s, n in _PROJ_SEGS]
    w_in_p = jnp.concatenate(cols, axis=1).astype(BF16)
    half = GH_W // 2
    w_gate = jnp.zeros((LANES, half), F32).at[GG_LOC:GG_LOC + GLA_RANK, 0:GQ].set(gla_w_gate[l]).astype(BF16)
    head_id = np.arange(D_MIX) // SLOT
    seg = jnp.asarray((head_id[:, None] == head_id[None, :]).astype(np.float32) / SLOT, dtype=BF16)
    vec_d = _pad_rows(jnp.stack([g_mix_pre[l], g_mix_post[l], _to_slots(g_head[l], 0),
                                 jnp.asarray(ML_LANES, F32)]))
    vec_512 = _pad_rows(jnp.stack([jnp.pad(gla_b_gate[l], (0, HQ)), jnp.pad(lb_all[l], (GQ, 0))]))
    gate_b = jnp.zeros((2 * MG,), F32)
    gate_b = gate_b.at[GATE_LOC:GATE_LOC + ML_H].set(ml_b_i[l])
    gate_b = gate_b.at[LANES + GATE_LOC:LANES + GATE_LOC + ML_H].set(ml_b_f[l])
    conv_p = _pad_rows(jnp.concatenate([_mqk(ml_conv_w[l]), _mqk(ml_conv_b[l])[None], gate_b[None]], axis=0))
    return dict(
        w_in=w_in_p, w_gate=w_gate, w_out=_to_slots(w_out[l], 0).astype(BF16), seg=seg, vec_d=vec_d,
        vec_512=vec_512, conv_p=conv_p,
        w_up=ffn_w_up[l].astype(BF16), w_down=ffn_w_down[l].astype(BF16),
        vec_ffn=_pad_rows(jnp.stack([g_ffn_pre[l], g_ffn_post[l]])),
        ffn_conv_p=_pad_rows(jnp.concatenate([ffn_conv_w[l], ffn_conv_b[l][None]], axis=0)),
    )


def _place(block, r0, c0, shape):
    return jnp.pad(block, ((0, 0), (r0, shape[0] - r0 - block.shape[1]), (c0, shape[1] - c0 - block.shape[2])))


def _pack_states(s_gla, s_hg, c_ml, n_ml, m_ml, buf_ml, buf_ffn):
    B = s_gla.shape[0]
    st = {}
    for i, tile in enumerate(GH_TILES):
        acc = 0.0
        for qoff, dk, slot, kind, h in tile["heads"]:
            s = s_gla[:, h] if kind == "g" else s_hg[:, h]
            acc = acc + _place(jnp.swapaxes(s, 1, 2), slot * SLOT, qoff, (TILE, tile["qw"]))
        st["s%d" % i] = acc
    for name, grp in zip(("ca", "cb"), ML_GROUPS):
        acc = 0.0
        for qoff, slot, h in grp["heads"]:
            acc = acc + _place(c_ml[:, h], qoff, slot * SLOT, (grp["qw"], TILE))
        st[name] = acc
    n_perm = jnp.concatenate([n_ml[:, h] for h in ML_ORDER], axis=-1)
    st["nml"] = jnp.pad(n_perm, ((0, 0), (0, MG - MQW)))[:, None, :]
    st["mml"] = jnp.pad(m_ml, ((0, 0), (0, LANES - ML_H)))[:, None, :]
    st["cbuf"] = jnp.pad(_mqk(buf_ml), ((0, 0), (SUBLANES - (ML_CONV - 1), 0), (0, 0)))
    fbuf = jnp.pad(buf_ffn, ((0, 0), (SUBLANES - (FFN_CONV - 1), 0), (0, 0)))
    return st, fbuf


def _unpack_states(st, fbuf):
    gl, hg, cm = {}, {}, {}
    for i, tile in enumerate(GH_TILES):
        s_t = st["s%d" % i]
        for qoff, dk, slot, kind, h in tile["heads"]:
            s = jnp.swapaxes(s_t[:, slot * SLOT:(slot + 1) * SLOT, qoff:qoff + dk], 1, 2)
            (gl if kind == "g" else hg)[h] = s
    for name, grp in zip(("ca", "cb"), ML_GROUPS):
        for qoff, slot, h in grp["heads"]:
            cm[h] = st[name][:, qoff:qoff + ML_DK, slot * SLOT:(slot + 1) * SLOT]
    n = st["nml"][:, 0, :]
    pos = {h: i for i, h in enumerate(ML_ORDER)}
    n_ml = jnp.stack([n[:, ML_DK * pos[h]:ML_DK * (pos[h] + 1)] for h in range(ML_H)], axis=1)
    m_ml = st["mml"][:, 0, :ML_H]
    buf_ml = _mqk_inv(st["cbuf"][:, SUBLANES - (ML_CONV - 1):, :])
    buf_ffn = fbuf[:, SUBLANES - (FFN_CONV - 1):, :]
    return (jnp.stack([gl[h] for h in range(GLA_H)], 1), jnp.stack([hg[h] for h in range(HG_H)], 1),
            jnp.stack([cm[h] for h in range(ML_H)], 1), n_ml, m_ml, buf_ml, buf_ffn)


def _block_sizes(B, T):
    if T >= 256:
        return 1, 256
    bb = 4 if B % 4 == 0 else 1
    return bb, T


def _run_trunk(x, states, weights):
    B, T, _ = x.shape
    bb, tb = _block_sizes(B, T)
    new = [[] for _ in range(7)]
    for l in range(DEPTH):
        st, fbuf = _pack_states(*(s[l] for s in states))
        x, st = _mixer_call(x, weights[l], st, bb=bb, tb=tb)
        x, fbuf = _ffn_call(x, weights[l], fbuf, bb=bb, tb=tb)
        for i, s in enumerate(_unpack_states(st, fbuf)):
            new[i].append(s)
    return x, [jnp.stack(s, axis=0) for s in new]


def kernel(x_prompt, x_sample, state_gla, state_hgrn, state_mlstm_C, state_mlstm_n,
           state_mlstm_m, cache_mlstm_conv, cache_ffn_conv, g_mix_pre, g_mix_post, g_ffn_pre,
           g_ffn_post, w_in, gla_w_gate, gla_b_gate, hgrn_lb, ml_conv_w, ml_conv_b, ml_b_i,
           ml_b_f, g_head, w_out, ffn_w_up, ffn_conv_w, ffn_conv_b, ffn_w_down):
    sm = jax.nn.softmax(hgrn_lb.astype(F32), axis=0)
    lb_all = jnp.cumsum(sm, axis=0) - sm[0:1]
    weights = [
        _layer_weights(l, lb_all, g_mix_pre, g_mix_post, g_ffn_pre, g_ffn_post, w_in, gla_w_gate,
                       gla_b_gate, ml_conv_w, ml_conv_b, ml_b_i, ml_b_f, g_head, w_out, ffn_w_up,
                       ffn_conv_w, ffn_conv_b, ffn_w_down)
        for l in range(DEPTH)]
    bp = x_prompt.shape[0]
    zero_states = (
        jnp.zeros((DEPTH, bp, GLA_H, GLA_DK, GLA_DV), F32),
        jnp.zeros((DEPTH, bp, HG_H, HG_DK, HG_DV), F32),
        jnp.zeros((DEPTH, bp, ML_H, ML_DK, ML_DV), F32),
        jnp.zeros((DEPTH, bp, ML_H, ML_DK), F32),
        jnp.zeros((DEPTH, bp, ML_H), F32),
        jnp.zeros((DEPTH, bp, ML_CONV - 1, 2 * ML_H * ML_DK), F32),
        jnp.zeros((DEPTH, bp, FFN_CONV - 1, 2 * D_FF), F32),
    )
    y_prompt, ps = _run_trunk(x_prompt, zero_states, weights)
    sample_states = (state_gla, state_hgrn, state_mlstm_C, state_mlstm_n, state_mlstm_m,
                     cache_mlstm_conv, cache_ffn_conv)
    y_sample, ss = _run_trunk(x_sample, sample_states, weights)
    return (y_prompt, y_sample, *ps, *ss)
```

```python
import functools

import numpy as np
import jax
import jax.numpy as jnp
from jax import lax
from jax.experimental import pallas as pl
from jax.experimental.pallas import tpu as pltpu

F32 = jnp.float32
BF16 = jnp.bfloat16
HIGHEST = lax.Precision.HIGHEST

D_MODEL = 1024
DEPTH = 2
GLA_H, GLA_DK, GLA_DV, GLA_RANK, GLA_GATE_NORM = 6, 32, 64, 16, 16.0
HG_H, HG_DK, HG_DV = 5, 64, 64
ML_H, ML_DK, ML_DV, ML_CONV = 5, 64, 64, 4
D_MIX = GLA_H * GLA_DV + HG_H * HG_DV + ML_H * ML_DV
D_FF = 2816
FFN_CONV = 3
EPS = 1e-6
NEG_BIG = -1e30

LANES = 128
SUBLANES = 8
TILE = 256
SLOT = 64
SUB = 16
MAX_CHUNK = 64
EXP_CLAMP = 80.0
VMEM_LIMIT = 56 * 1024 * 1024

GQ, HQ = GLA_H * GLA_DK, HG_H * HG_DK
GV, HV, MV = GLA_H * GLA_DV, HG_H * HG_DV, ML_H * ML_DV
MQW = ML_H * ML_DK
GH_W = GQ + HQ
MG = 384
C_Q = 0
C_K = C_Q + GH_W
C_V = C_K + GH_W
C_OG = C_V + D_MIX
C_MQ = C_OG + D_MIX
C_MK = C_MQ + MG
N_PROJ = C_MK + MG
SMALL_OFF = 256
GG_LOC = MQW - SMALL_OFF
GATE_LOC = GG_LOC + GLA_RANK
ML_ORDER = (1, 2, 3, 4, 0)

_SPLITS = (GQ, GQ, GV, GLA_RANK, GV, HQ, HQ, HV, HV, MQW, MQW, MV, ML_H, ML_H, MV)
_NAMES = ("gq", "gk", "gv", "gg", "gr", "hq", "hf", "hi", "hg", "mq", "mk", "mv", "mi", "mf", "mo")
_SRC = {}
_acc = 0
for _n, _s in zip(_NAMES, _SPLITS):
    _SRC[_n] = _acc
    _acc += _s
N_IN = _acc

SLOT_HEADS = (("g", 0), ("g", 1), ("g", 2), ("g", 3),
              ("g", 4), ("g", 5), ("h", 0), ("m", 0),
              ("h", 1), ("h", 2), ("h", 3), ("h", 4),
              ("m", 1), ("m", 2), ("m", 3), ("m", 4))
_CAT_OFF = {"g": 0, "h": GV, "m": GV + HV}
CAT_PERM = np.concatenate([_CAT_OFF[k] + SLOT * h + np.arange(SLOT) for k, h in SLOT_HEADS])
ML_LANES = np.concatenate([np.full(SLOT, k == "m") for k, _ in SLOT_HEADS])

GH_TILES = (
    dict(q0=0, qw=128, vt=0, heads=tuple((32 * r, GLA_DK, r, "g", r) for r in range(4))),
    dict(q0=128, qw=128, vt=1, heads=((0, GLA_DK, 0, "g", 4), (32, GLA_DK, 1, "g", 5), (64, HG_DK, 2, "h", 0))),
    dict(q0=256, qw=256, vt=2, heads=tuple((64 * r, HG_DK, r, "h", r + 1) for r in range(4))),
)
ML_GROUPS = (
    dict(q0=0, qw=256, vt=3, heads=tuple((64 * r, r, r + 1) for r in range(4))),
    dict(q0=256, qw=128, vt=1, heads=((0, 3, 0),)),
)


def _proj_segments():
    segs = [(_SRC["gq"], GQ), (_SRC["hq"], HQ), (_SRC["gk"], GQ), (_SRC["hf"], HQ)]
    for names in (("gv", "hi", "mv"), ("gr", "hg", "mo")):
        src = dict(zip("ghm", names))
        for kind, h in SLOT_HEADS:
            segs.append((_SRC[src[kind]] + SLOT * h, SLOT))
    pad = MG - MQW - GLA_RANK - ML_H
    for name, gate in (("mq", "mi"), ("mk", "mf")):
        for h in ML_ORDER:
            segs.append((_SRC[name] + ML_DK * h, ML_DK))
        segs.append((_SRC["gg"], GLA_RANK) if name == "mq" else (None, GLA_RANK))
        segs.append((_SRC[gate], ML_H))
        segs.append((None, pad))
    merged = []
    for s, n in segs:
        if merged and s is not None and merged[-1][0] is not None and merged[-1][0] + merged[-1][1] == s:
            merged[-1] = (merged[-1][0], merged[-1][1] + n)
        else:
            merged.append((s, n))
    assert sum(n for _, n in merged) == N_PROJ
    return merged


_PROJ_SEGS = _proj_segments()
_PROJ_IDX = np.concatenate([np.arange(s, s + n) if s is not None else np.zeros(n, np.int64)
                            for s, n in _PROJ_SEGS]).astype(np.int32)
_PROJ_MSK = np.concatenate([np.full(n, 0.0 if s is None else 1.0, np.float32) for s, n in _PROJ_SEGS])


def _sigmoid(x):
    return 1.0 / (1.0 + jnp.exp(-x))


def _log_sigmoid(x):
    return jnp.minimum(x, 0.0) - jnp.log(1.0 + jnp.exp(-jnp.abs(x)))


def _rms(x, g):
    return x * lax.rsqrt(jnp.mean(x * x, axis=-1, keepdims=True) + EPS) * g


def _split3(x):
    hi = x.astype(BF16)
    r1 = x - hi.astype(F32)
    mid = r1.astype(BF16)
    lo = (r1 - mid.astype(F32)).astype(BF16)
    return hi, mid, lo


def _in_range(idx, lo, width):
    return (idx >= lo) & (idx < lo + width)


def _dot(a, b, precision=None):
    return jnp.dot(a, b, preferred_element_type=F32, precision=precision)


def _dot_nt(a, b):
    return lax.dot_general(a, b, (((1,), (1,)), ((), ())), preferred_element_type=F32)


def _dot_tn(a, b, precision=None):
    return lax.dot_general(a, b, (((0,), (0,)), ((), ())), preferred_element_type=F32,
                           precision=precision)


def _mixer_kernel(x_ref, win_ref, wg_ref, wout_ref, seg_ref, vd_ref, v512_ref, cp_ref,
                  s0_in, s1_in, s2_in, ca_in, cb_in, nml_in, mml_in, cbuf_in,
                  y_ref, s0_ref, s1_ref, s2_ref, ca_ref, cb_ref, nml_ref, mml_ref, cbuf_ref,
                  p_ref, q_ref, k_ref, la_ref, mq_ref, mk_ref, ig_ref, lf_ref, xp_ref, cat_ref,
                  *, bb, tb, chunk):
    t = pl.program_id(1)
    rows = bb * tb
    nb = chunk // SUB
    state_pairs = ((s0_in, s0_ref), (s1_in, s1_ref), (s2_in, s2_ref), (ca_in, ca_ref),
                   (cb_in, cb_ref), (nml_in, nml_ref), (mml_in, mml_ref), (cbuf_in, cbuf_ref))

    @pl.when(t == 0)
    def _():
        for src, dst in state_pairs:
            dst[...] = src[...]

    x = x_ref[...].reshape(rows, D_MODEL)
    xn = _rms(x, vd_ref[0:1, :]).astype(BF16)
    p_ref[...] = _dot(xn, win_ref[...])

    half = GH_W // 2
    small_q = p_ref[:, C_MQ + SMALL_OFF:C_MQ + MG]
    z = _dot(small_q.astype(BF16), wg_ref[...]) + v512_ref[0:1, 0:half]
    la_gla = _log_sigmoid(z) * (1.0 / GLA_GATE_NORM)
    is_gla = lax.broadcasted_iota(jnp.int32, (1, half), 1) < GQ
    for c0 in (0, half):
        qraw = p_ref[:, C_Q + c0:C_Q + c0 + half]
        fr = p_ref[:, C_K + c0:C_K + c0 + half]
        lb = v512_ref[1:2, c0:c0 + half]
        q_h = qraw * _sigmoid(qraw) * (HG_DK ** -0.5)
        la_h = jnp.log(lb + (1.0 - lb) * _sigmoid(fr))
        k_h = (1.0 - lb) * _sigmoid(-fr)
        if c0 == 0:
            q_ref[:, 0:half] = jnp.where(is_gla, qraw * (GLA_DK ** -0.5), q_h)
            k_ref[:, 0:half] = jnp.where(is_gla, fr, k_h)
            la_ref[:, 0:half] = jnp.where(is_gla, la_gla, la_h)
        else:
            q_ref[:, half:] = q_h
            k_ref[:, half:] = k_h
            la_ref[:, half:] = la_h

    pre = SUBLANES
    xp_ref[:, 0:pre, :] = cbuf_ref[...]
    xp_ref[:, pre:pre + tb, :] = p_ref[:, C_MQ:C_MQ + 2 * MG].reshape(bb, tb, 2 * MG)
    conv = cp_ref[ML_CONV:ML_CONV + 1, :].reshape(1, 1, 2 * MG)
    for j in range(ML_CONV):
        off = pre - (ML_CONV - 1) + j
        conv = conv + xp_ref[:, off:off + tb, :] * cp_ref[j:j + 1, :].reshape(1, 1, 2 * MG)
    cbuf_ref[...] = xp_ref[:, tb:tb + pre, :]
    qk = (conv * _sigmoid(conv)).reshape(rows, 2 * MG)
    mq_ref[...] = qk[:, 0:MG]
    mk_ref[...] = qk[:, MG:] * (ML_DK ** -0.5)
    ig_ref[...] = small_q + cp_ref[ML_CONV + 1:ML_CONV + 2, 0:LANES]
    lf_ref[...] = _log_sigmoid(p_ref[:, C_MK + SMALL_OFF:C_MK + MG] + cp_ref[ML_CONV + 1:ML_CONV + 2, LANES:2 * LANES])

    sub_sh = SUB.bit_length() - 1
    chunk_sh = chunk.bit_length() - 1
    assert (1 << sub_sh) == SUB and (1 << chunk_sh) == chunk
    ri = lax.broadcasted_iota(jnp.int32, (chunk, chunk), 0)
    ci = lax.broadcasted_iota(jnp.int32, (chunk, chunk), 1)
    causal = ci <= ri
    tri = causal.astype(BF16)
    tri_blk = (causal & ((ri >> sub_sh) == (ci >> sub_sh))).astype(BF16)
    tri3 = jnp.concatenate([tri] * 3, axis=1)
    cum3 = jnp.concatenate([tri3, jnp.concatenate([tri_blk] * 3, axis=1)], axis=0)
    li = lax.broadcasted_iota(jnp.int32, (LANES, 3 * LANES), 0)
    lj = lax.broadcasted_iota(jnp.int32, (LANES, 3 * LANES), 1) & (LANES - 1)
    eye3 = (li == lj).astype(BF16)
    max_heads = TILE // SLOT
    sr = lax.broadcasted_iota(jnp.int32, (max_heads * chunk, nb * chunk), 0) & (chunk - 1)
    sc = lax.broadcasted_iota(jnp.int32, (max_heads * chunk, nb * chunk), 1)
    slab, col = sc >> chunk_sh, sc & (chunk - 1)
    off_ok = (slab < nb - 1) & ((sr >> sub_sh) == slab + 1) & (col < (slab + 1) * SUB)
    diag_ok = (slab == nb - 1) & ((sr >> sub_sh) == (col >> sub_sh)) & (col <= sr)
    row_id = lax.broadcasted_iota(jnp.int32, (chunk, 1), 0)
    sub_pos = row_id & (SUB - 1)
    sub_last = sub_pos == SUB - 1
    lane_v = lax.broadcasted_iota(jnp.int32, (1, TILE), 1)
    lane_m = lax.broadcasted_iota(jnp.int32, (1, LANES), 1)
    gh_refs = (s0_ref, s1_ref, s2_ref)
    ml_refs = (ca_ref, cb_ref)
    chunks_per_stream = tb // chunk

    def chunk_body(ic, carry):
        b_i = ic // chunks_per_stream
        rs = pl.ds(pl.multiple_of(ic * chunk, chunk), chunk)
        v_all = p_ref[rs, C_V:C_V + D_MIX].astype(BF16)
        out_tiles = [None] * (D_MIX // TILE)

        def add_out(vt, val):
            out_tiles[vt] = val if out_tiles[vt] is None else out_tiles[vt] + val

        la = la_ref[rs, :]
        cums = _dot(cum3, jnp.concatenate(_split3(la), axis=0))
        b = cums[0:chunk]
        bq = cums[chunk:]
        q = q_ref[rs, :]
        k = k_ref[rs, :]
        b_last = b[chunk - 1:chunk, :]
        q_in = (q * jnp.exp(bq)).astype(BF16)
        q_st = (q * jnp.exp(b)).astype(BF16)
        k_st = (k * jnp.exp(b_last - b)).astype(BF16)
        dec = jnp.exp(b_last)
        pieces = []
        for i in range(1, nb):
            r_i = b[i * SUB - 1:i * SUB, :]
            kt = k * jnp.exp(jnp.minimum(r_i - b, 0.0))
            pieces.append(jnp.where(row_id < i * SUB, kt, 0.0))
        pieces.append(k * jnp.exp(jnp.minimum(-bq, EXP_CLAMP)))
        worst = jnp.max(jnp.where(sub_last, -bq, 0.0))
        score_sel = off_ok | (diag_ok & (worst <= EXP_CLAMP))
        k_stack = jnp.concatenate(pieces, axis=0).astype(BF16)
        v_rep = jnp.concatenate([v_all] * nb, axis=0)

        for tile, s_ref in zip(GH_TILES, gh_refs):
            q0, qw, vt, heads = tile["q0"], tile["qw"], tile["vt"], tile["heads"]
            nr = len(heads)
            qs = slice(q0, q0 + qw)
            vs = slice(vt * TILE, (vt + 1) * TILE)
            lane_q = lax.broadcasted_iota(jnp.int32, (1, qw), 1)
            row_s = lax.broadcasted_iota(jnp.int32, (TILE, 1), 0)
            qmasks = [_in_range(lane_q, qoff, dk) for qoff, dk, _, _, _ in heads]
            lhs_in = jnp.concatenate([jnp.where(m, q_in[:, qs], 0) for m in qmasks], axis=0)
            lhs_st = jnp.concatenate([jnp.where(m, q_st[:, qs], 0) for m in qmasks], axis=0)
            scores = _dot_nt(lhs_in, k_stack[:, qs])
            scores = jnp.where(score_sel[0:nr * chunk], scores, 0.0).astype(BF16)
            s_t = s_ref[b_i]
            pv = _dot(scores, v_rep[:, vs]) + _dot_nt(lhs_st, s_t.astype(BF16))
            block = None
            for r, (qoff, dk, slot, _, _) in enumerate(heads):
                add_out(vt, jnp.where(_in_range(lane_v, slot * SLOT, SLOT), pv[r * chunk:(r + 1) * chunk], 0.0))
                m = _in_range(row_s, slot * SLOT, SLOT) & qmasks[r]
                block = m if block is None else block | m
            upd = _dot_tn(v_all[:, vs], k_st[:, qs])
            s_ref[b_i] = s_t * dec[:, qs] + jnp.where(block, upd, 0.0)

        lf = lf_ref[rs, :]
        ig = ig_ref[rs, :]
        bm = _dot(tri3, jnp.concatenate(_split3(lf), axis=0))
        u_t = _dot_nt(eye3, jnp.concatenate(_split3(ig - bm), axis=1))
        m_all = mml_ref[b_i]
        m_new = m_all
        for grp, c_ref in zip(ML_GROUPS, ml_refs):
            q0, qw, vt, heads = grp["q0"], grp["qw"], grp["vt"], grp["heads"]
            qs = slice(q0, q0 + qw)
            vs = slice(vt * TILE, (vt + 1) * TILE)
            lane_q = lax.broadcasted_iota(jnp.int32, (1, qw), 1)
            row_q = lax.broadcasted_iota(jnp.int32, (qw, 1), 0)
            mq = mq_ref[rs, qs]
            mk = mk_ref[rs, qs]
            n_row = nml_ref[b_i, :, qs]
            lhs, w_st, w0_st, mi_st = [], [], [], []
            wk_l = jnp.zeros((chunk, qw), F32)
            dc_q = jnp.zeros((1, qw), F32)
            dc_v = jnp.zeros((1, TILE), F32)
            block = None
            for qoff, slot, h in heads:
                qm = _in_range(lane_q, qoff, ML_DK)
                vm = _in_range(lane_v, slot * SLOT, SLOT)
                gl = GATE_LOC + h
                bcol = bm[:, gl:gl + 1]
                m_prev = m_all[:, h:h + 1]
                dlog = jnp.where(causal, bcol + u_t[gl:gl + 1, :], NEG_BIG)
                g = bcol + m_prev
                m_i = jnp.maximum(g, jnp.max(dlog, axis=1, keepdims=True))
                w_st.append(jnp.where(causal, jnp.exp(dlog - m_i), 0.0))
                w0_st.append(jnp.exp(g - m_i))
                mi_st.append(m_i)
                lhs.append(jnp.where(qm, mq, 0.0))
                m_last = m_i[chunk - 1:chunk, :]
                bl = bcol[chunk - 1:chunk, :]
                wk = jnp.exp(bl - bcol + ig[:, gl:gl + 1] - m_last)
                dc = jnp.exp(bl + m_prev - m_last)
                wk_l = jnp.where(qm, wk, wk_l)
                dc_q = jnp.where(qm, dc, dc_q)
                dc_v = jnp.where(vm, dc, dc_v)
                m_new = jnp.where(lane_m == h, m_last, m_new)
                bm_ = _in_range(row_q, qoff, ML_DK) & vm
                block = bm_ if block is None else block | bm_
            lhs = jnp.concatenate(lhs, axis=0)
            w_st = jnp.concatenate(w_st, axis=0)
            w0_st = jnp.concatenate(w0_st, axis=0)
            mi_st = jnp.concatenate(mi_st, axis=0)
            lhs_b = lhs.astype(BF16)
            s = _dot_nt(lhs_b, mk.astype(BF16)) * w_st
            c_st = c_ref[b_i]
            num = _dot(s.astype(BF16), v_all[:, vs]) + w0_st * _dot(lhs_b, c_st.astype(BF16))
            den = (jnp.sum(s, axis=1, keepdims=True)
                   + w0_st * jnp.sum(lhs * n_row, axis=1, keepdims=True))
            hout = num / jnp.maximum(jnp.abs(den), jnp.exp(-mi_st))
            for r, (qoff, slot, h) in enumerate(heads):
                add_out(vt, jnp.where(_in_range(lane_v, slot * SLOT, SLOT), hout[r * chunk:(r + 1) * chunk], 0.0))
            kw = mk * wk_l
            c_ref[b_i] = c_st * dc_v + jnp.where(block, _dot_tn(kw.astype(BF16), v_all[:, vs]), 0.0)
            nml_ref[b_i, :, qs] = n_row * dc_q + jnp.sum(kw, axis=0, keepdims=True)
        mml_ref[b_i] = m_new

        for vt, val in enumerate(out_tiles):
            cat_ref[rs, vt * TILE:(vt + 1) * TILE] = val

        @pl.when(worst > EXP_CLAMP)
        def _():
            hl = lax.broadcasted_iota(jnp.int32, (3 * GH_W, LANES), 0) & (GH_W - 1)
            hc = lax.broadcasted_iota(jnp.int32, (3 * GH_W, LANES), 1)
            head_of_lane = jnp.where(hl < GQ, hl >> (GLA_DK.bit_length() - 1),
                                     GLA_H + ((hl - GQ) >> (HG_DK.bit_length() - 1)))
            seg_q3 = (head_of_lane == hc).astype(BF16)
            bands = []
            for s in range(SUB):
                k_s = k if s == 0 else pltpu.roll(k, s, 0)
                b_s = b if s == 0 else pltpu.roll(b, s, 0)
                pair = q * k_s * jnp.exp(jnp.minimum(b - b_s, 0.0))
                pair = jnp.where(sub_pos >= s, pair, 0.0)
                bands.append(_dot(jnp.concatenate(_split3(pair), axis=1), seg_q3))
            for tile in GH_TILES:
                vs = slice(tile["vt"] * TILE, (tile["vt"] + 1) * TILE)
                corr = None
                for qoff, dk, slot, kind, h in tile["heads"]:
                    hcol = h if kind == "g" else GLA_H + h
                    d = jnp.zeros((chunk, chunk), F32)
                    for s in range(SUB):
                        d = jnp.where(ci == ri - s, bands[s][:, hcol:hcol + 1], d)
                    o = _dot(d.astype(BF16), v_all[:, vs])
                    o = jnp.where(_in_range(lane_v, slot * SLOT, SLOT), o, 0.0)
                    corr = o if corr is None else corr + o
                cat_ref[rs, vs] += corr

        return carry

    lax.fori_loop(0, rows // chunk, chunk_body, 0)

    o = cat_ref[...]
    ms = _dot((o * o).astype(BF16), seg_ref[...])
    og = p_ref[:, C_OG:C_OG + D_MIX]
    gate = _sigmoid(og) * jnp.where(vd_ref[3:4, :] > 0.5, 1.0, og)
    cat = (o * lax.rsqrt(ms + EPS) * vd_ref[2:3, :] * gate).astype(BF16)
    hm = _dot(cat, wout_ref[...])
    y = x + _rms(hm, vd_ref[1:2, :])
    y_ref[...] = y.reshape(bb, tb, D_MODEL)


def _const_spec(shape):
    nd = len(shape)
    return pl.BlockSpec(shape, lambda b, t: (0,) * nd, pipeline_mode=pl.Buffered(1))


def _state_spec(shape, bb):
    nd = len(shape)
    return pl.BlockSpec((bb,) + tuple(shape[1:]), lambda b, t: (b,) + (0,) * (nd - 1))


_MIXER_STATES = ("s0", "s1", "s2", "ca", "cb", "nml", "mml", "cbuf")


def _mixer_call(x, lw, st, *, bb, tb):
    B, T, _ = x.shape
    chunk = min(MAX_CHUNK, tb)
    rows = bb * tb
    grid = (B // bb, T // tb)
    x_spec = pl.BlockSpec((bb, tb, D_MODEL), lambda b, t: (b, t, 0))
    consts = (lw["w_in"], lw["w_gate"], lw["w_out"], lw["seg"], lw["vec_d"], lw["vec_512"], lw["conv_p"])
    states = tuple(st[n] for n in _MIXER_STATES)
    in_specs = ([x_spec] + [_const_spec(c.shape) for c in consts]
                + [_state_spec(s.shape, bb) for s in states])
    out_shape = ([jax.ShapeDtypeStruct(x.shape, F32)]
                 + [jax.ShapeDtypeStruct(s.shape, F32) for s in states])
    out_specs = [x_spec] + [_state_spec(s.shape, bb) for s in states]
    scratch = [
        pltpu.VMEM((rows, N_PROJ), F32),
        pltpu.VMEM((rows, GH_W), F32),
        pltpu.VMEM((rows, GH_W), F32),
        pltpu.VMEM((rows, GH_W), F32),
        pltpu.VMEM((rows, MG), F32),
        pltpu.VMEM((rows, MG), F32),
        pltpu.VMEM((rows, LANES), F32),
        pltpu.VMEM((rows, LANES), F32),
        pltpu.VMEM((bb, tb + SUBLANES, 2 * MG), F32),
        pltpu.VMEM((rows, D_MIX), F32),
    ]
    outs = pl.pallas_call(
        functools.partial(_mixer_kernel, bb=bb, tb=tb, chunk=chunk),
        grid=grid, in_specs=in_specs, out_specs=out_specs, out_shape=out_shape,
        scratch_shapes=scratch,
        compiler_params=pltpu.CompilerParams(
            dimension_semantics=("arbitrary", "arbitrary"), vmem_limit_bytes=VMEM_LIMIT),
        name="mixer",
    )(x, *consts, *states)
    return outs[0], dict(zip(_MIXER_STATES, outs[1:]))


def _ffn_kernel(x_ref, wup_ref, wdn_ref, vd_ref, cp_ref, fbuf_in, y_ref, fbuf_ref, xp_ref,
                *, bb, tb):
    t = pl.program_id(1)
    rows = bb * tb
    pre = SUBLANES

    @pl.when(t == 0)
    def _():
        fbuf_ref[...] = fbuf_in[...]

    x = x_ref[...].reshape(rows, D_MODEL)
    xn = _rms(x, vd_ref[0:1, :]).astype(BF16)
    up = _dot(xn, wup_ref[...])
    xp_ref[:, 0:pre, :] = fbuf_ref[...]
    xp_ref[:, pre:pre + tb, :] = up.reshape(bb, tb, 2 * D_FF)
    conv = cp_ref[FFN_CONV:FFN_CONV + 1, :].reshape(1, 1, 2 * D_FF)
    for j in range(FFN_CONV):
        off = pre - (FFN_CONV - 1) + j
        conv = conv + xp_ref[:, off:off + tb, :] * cp_ref[j:j + 1, :].reshape(1, 1, 2 * D_FF)
    fbuf_ref[...] = xp_ref[:, tb:tb + pre, :]
    conv = conv.reshape(rows, 2 * D_FF)
    gate = conv[:, 0:D_FF]
    val = conv[:, D_FF:]
    c0 = 0.7978845608028654
    gelu = 0.5 * gate * (1.0 + jnp.tanh(c0 * (gate + 0.044715 * gate * gate * gate)))
    hmid = (gelu * val).astype(BF16)
    hd = _dot(hmid, wdn_ref[...])
    y = x + _rms(hd, vd_ref[1:2, :])
    y_ref[...] = y.reshape(bb, tb, D_MODEL)


def _ffn_call(x, lw, fbuf, *, bb, tb):
    B, T, _ = x.shape
    grid = (B // bb, T // tb)
    x_spec = pl.BlockSpec((bb, tb, D_MODEL), lambda b, t: (b, t, 0))
    consts = (lw["w_up"], lw["w_down"], lw["vec_ffn"], lw["ffn_conv_p"])
    in_specs = [x_spec] + [_const_spec(c.shape) for c in consts] + [_state_spec(fbuf.shape, bb)]
    out_shape = [jax.ShapeDtypeStruct(x.shape, F32), jax.ShapeDtypeStruct(fbuf.shape, F32)]
    out_specs = [x_spec, _state_spec(fbuf.shape, bb)]
    y, fnew = pl.pallas_call(
        functools.partial(_ffn_kernel, bb=bb, tb=tb),
        grid=grid, in_specs=in_specs, out_specs=out_specs, out_shape=out_shape,
        scratch_shapes=[pltpu.VMEM((bb, tb + SUBLANES, 2 * D_FF), F32)],
        compiler_params=pltpu.CompilerParams(
            dimension_semantics=("arbitrary", "arbitrary"), vmem_limit_bytes=VMEM_LIMIT),
        name="ffn",
    )(x, *consts, fbuf)
    return y, fnew


def _pad_rows(a, n=SUBLANES):
    return jnp.pad(a, ((0, n - a.shape[0]), (0, 0)))


def _mqk(a):
    lead = a.shape[:-1]
    parts = []
    for half in (a[..., :MQW], a[..., MQW:]):
        hh = half.reshape(lead + (ML_H, ML_DK))
        parts += [hh[..., h, :] for h in ML_ORDER] + [jnp.zeros(lead + (MG - MQW,), a.dtype)]
    return jnp.concatenate(parts, axis=-1)


def _mqk_inv(a):
    inv = np.argsort(np.asarray(ML_ORDER))
    parts = []
    for base in (0, MG):
        parts += [a[..., base + ML_DK * int(p):base + ML_DK * (int(p) + 1)] for p in inv]
    return jnp.concatenate(parts, axis=-1)


def _to_slots(a, axis):
    parts = [lax.slice_in_dim(a, _CAT_OFF[k] + SLOT * h, _CAT_OFF[k] + SLOT * (h + 1), axis=axis)
             for k, h in SLOT_HEADS]
    return jnp.concatenate(parts, axis=axis)


def _layer_weights(l, lb_all, g_mix_pre, g_mix_post, g_ffn_pre, g_ffn_post, w_in, gla_w_gate,
                   gla_b_gate, ml_conv_w, ml_conv_b, ml_b_i, ml_b_f, g_head, w_out, ffn_w_up,
                   ffn_conv_w, ffn_conv_b, ffn_w_down):
    w = w_in[l]
    w_in_p = (jnp.take(w, jnp.asarray(_PROJ_IDX), axis=1) * jnp.asarray(_PROJ_MSK)).astype(BF16)
    half = GH_W // 2
    w_gate = jnp.zeros((LANES, half), F32).at[GG_LOC:GG_LOC + GLA_RANK, 0:GQ].set(gla_w_gate[l]).astype(BF16)
    head_id = np.arange(D_MIX) // SLOT
    seg = jnp.asarray((head_id[:, None] == head_id[None, :]).astype(np.float32) / SLOT, dtype=BF16)
    vec_d = _pad_rows(jnp.stack([g_mix_pre[l], g_mix_post[l], _to_slots(g_head[l], 0),
                                 jnp.asarray(ML_LANES, F32)]))
    vec_512 = _pad_rows(jnp.stack([jnp.pad(gla_b_gate[l], (0, HQ)), jnp.pad(lb_all[l], (GQ, 0))]))
    gate_b = jnp.zeros((2 * MG,), F32)
    gate_b = gate_b.at[GATE_LOC:GATE_LOC + ML_H].set(ml_b_i[l])
    gate_b = gate_b.at[LANES + GATE_LOC:LANES + GATE_LOC + ML_H].set(ml_b_f[l])
    conv_p = _pad_rows(jnp.concatenate([_mqk(ml_conv_w[l]), _mqk(ml_conv_b[l])[None], gate_b[None]], axis=0))
    return dict(
        w_in=w_in_p, w_gate=w_gate, w_out=jnp.take(w_out[l], jnp.asarray(CAT_PERM, jnp.int32), axis=0).astype(BF16), seg=seg, vec_d=vec_d,
        vec_512=vec_512, conv_p=conv_p,
        w_up=ffn_w_up[l].astype(BF16), w_down=ffn_w_down[l].astype(BF16),
        vec_ffn=_pad_rows(jnp.stack([g_ffn_pre[l], g_ffn_post[l]])),
        ffn_conv_p=_pad_rows(jnp.concatenate([ffn_conv_w[l], ffn_conv_b[l][None]], axis=0)),
    )


def _place(block, r0, c0, shape):
    return jnp.pad(block, ((0, 0), (r0, shape[0] - r0 - block.shape[1]), (c0, shape[1] - c0 - block.shape[2])))


def _pack_states(s_gla, s_hg, c_ml, n_ml, m_ml, buf_ml, buf_ffn):
    B = s_gla.shape[0]
    st = {}
    for i, tile in enumerate(GH_TILES):
        acc = 0.0
        for qoff, dk, slot, kind, h in tile["heads"]:
            s = s_gla[:, h] if kind == "g" else s_hg[:, h]
            acc = acc + _place(jnp.swapaxes(s, 1, 2), slot * SLOT, qoff, (TILE, tile["qw"]))
        st["s%d" % i] = acc
    for name, grp in zip(("ca", "cb"), ML_GROUPS):
        acc = 0.0
        for qoff, slot, h in grp["heads"]:
            acc = acc + _place(c_ml[:, h], qoff, slot * SLOT, (grp["qw"], TILE))
        st[name] = acc
    n_perm = jnp.concatenate([n_ml[:, h] for h in ML_ORDER], axis=-1)
    st["nml"] = jnp.pad(n_perm, ((0, 0), (0, MG - MQW)))[:, None, :]
    st["mml"] = jnp.pad(m_ml, ((0, 0), (0, LANES - ML_H)))[:, None, :]
    st["cbuf"] = jnp.pad(_mqk(buf_ml), ((0, 0), (SUBLANES - (ML_CONV - 1), 0), (0, 0)))
    fbuf = jnp.pad(buf_ffn, ((0, 0), (SUBLANES - (FFN_CONV - 1), 0), (0, 0)))
    return st, fbuf


def _unpack_states(st, fbuf):
    gl, hg, cm = {}, {}, {}
    for i, tile in enumerate(GH_TILES):
        s_t = st["s%d" % i]
        for qoff, dk, slot, kind, h in tile["heads"]:
            s = jnp.swapaxes(s_t[:, slot * SLOT:(slot + 1) * SLOT, qoff:qoff + dk], 1, 2)
            (gl if kind == "g" else hg)[h] = s
    for name, grp in zip(("ca", "cb"), ML_GROUPS):
        for qoff, slot, h in grp["heads"]:
            cm[h] = st[name][:, qoff:qoff + ML_DK, slot * SLOT:(slot + 1) * SLOT]
    n = st["nml"][:, 0, :]
    pos = {h: i for i, h in enumerate(ML_ORDER)}
    n_ml = jnp.stack([n[:, ML_DK * pos[h]:ML_DK * (pos[h] + 1)] for h in range(ML_H)], axis=1)
    m_ml = st["mml"][:, 0, :ML_H]
    buf_ml = _mqk_inv(st["cbuf"][:, SUBLANES - (ML_CONV - 1):, :])
    buf_ffn = fbuf[:, SUBLANES - (FFN_CONV - 1):, :]
    return (jnp.stack([gl[h] for h in range(GLA_H)], 1), jnp.stack([hg[h] for h in range(HG_H)], 1),
            jnp.stack([cm[h] for h in range(ML_H)], 1), n_ml, m_ml, buf_ml, buf_ffn)


def _block_sizes(B, T):
    if T >= 256:
        return 1, 256
    bb = 4 if B % 4 == 0 else 1
    return bb, T


def _run_trunk(x, states, weights):
    B, T, _ = x.shape
    bb, tb = _block_sizes(B, T)
    new = [[] for _ in range(7)]
    for l in range(DEPTH):
        st, fbuf = _pack_states(*(s[l] for s in states))
        x, st = _mixer_call(x, weights[l], st, bb=bb, tb=tb)
        x, fbuf = _ffn_call(x, weights[l], fbuf, bb=bb, tb=tb)
        for i, s in enumerate(_unpack_states(st, fbuf)):
            new[i].append(s)
    return x, [jnp.stack(s, axis=0) for s in new]


def kernel(x_prompt, x_sample, state_gla, state_hgrn, state_mlstm_C, state_mlstm_n,
           state_mlstm_m, cache_mlstm_conv, cache_ffn_conv, g_mix_pre, g_mix_post, g_ffn_pre,
           g_ffn_post, w_in, gla_w_gate, gla_b_gate, hgrn_lb, ml_conv_w, ml_conv_b, ml_b_i,
           ml_b_f, g_head, w_out, ffn_w_up, ffn_conv_w, ffn_conv_b, ffn_w_down):
    sm = jax.nn.softmax(hgrn_lb.astype(F32), axis=0)
    lb_all = jnp.cumsum(sm, axis=0) - sm[0:1]
    weights = [
        _layer_weights(l, lb_all, g_mix_pre, g_mix_post, g_ffn_pre, g_ffn_post, w_in, gla_w_gate,
                       gla_b_gate, ml_conv_w, ml_conv_b, ml_b_i, ml_b_f, g_head, w_out, ffn_w_up,
                       ffn_conv_w, ffn_conv_b, ffn_w_down)
        for l in range(DEPTH)]
    bp = x_prompt.shape[0]
    zero_states = (
        jnp.zeros((DEPTH, bp, GLA_H, GLA_DK, GLA_DV), F32),
        jnp.zeros((DEPTH, bp, HG_H, HG_DK, HG_DV), F32),
        jnp.zeros((DEPTH, bp, ML_H, ML_DK, ML_DV), F32),
        jnp.zeros((DEPTH, bp, ML_H, ML_DK), F32),
        jnp.zeros((DEPTH, bp, ML_H), F32),
        jnp.zeros((DEPTH, bp, ML_CONV - 1, 2 * ML_H * ML_DK), F32),
        jnp.zeros((DEPTH, bp, FFN_CONV - 1, 2 * D_FF), F32),
    )
    y_prompt, ps = _run_trunk(x_prompt, zero_states, weights)
    sample_states = (state_gla, state_hgrn, state_mlstm_C, state_mlstm_n, state_mlstm_m,
                     cache_mlstm_conv, cache_ffn_conv)
    y_sample, ss = _run_trunk(x_sample, sample_states, weights)
    return (y_prompt, y_sample, *ps, *ss)
```

```python
import functools

import numpy as np
import jax
import jax.numpy as jnp
from jax import lax
from jax.experimental import pallas as pl
from jax.experimental.pallas import tpu as pltpu

F32 = jnp.float32
BF16 = jnp.bfloat16
HIGHEST = lax.Precision.HIGHEST

D_MODEL = 1024
DEPTH = 2
GLA_H, GLA_DK, GLA_DV, GLA_RANK, GLA_GATE_NORM = 6, 32, 64, 16, 16.0
HG_H, HG_DK, HG_DV = 5, 64, 64
ML_H, ML_DK, ML_DV, ML_CONV = 5, 64, 64, 4
D_MIX = GLA_H * GLA_DV + HG_H * HG_DV + ML_H * ML_DV
D_FF = 2816
FFN_CONV = 3
EPS = 1e-6
NEG_BIG = -1e30

LANES = 128
SUBLANES = 8
TILE = 256
SLOT = 64
SUB = 16
MAX_CHUNK = 64
EXP_CLAMP = 80.0
VMEM_LIMIT = 56 * 1024 * 1024

GQ, HQ = GLA_H * GLA_DK, HG_H * HG_DK
GV, HV, MV = GLA_H * GLA_DV, HG_H * HG_DV, ML_H * ML_DV
MQW = ML_H * ML_DK
GH_W = GQ + HQ
MG = 384
C_Q = 0
C_K = C_Q + GH_W
C_V = C_K + GH_W
C_OG = C_V + D_MIX
C_MQ = C_OG + D_MIX
C_MK = C_MQ + MG
N_PROJ = C_MK + MG
SMALL_OFF = 256
GG_LOC = MQW - SMALL_OFF
GATE_LOC = GG_LOC + GLA_RANK
ML_ORDER = (1, 2, 3, 4, 0)

_SPLITS = (GQ, GQ, GV, GLA_RANK, GV, HQ, HQ, HV, HV, MQW, MQW, MV, ML_H, ML_H, MV)
_NAMES = ("gq", "gk", "gv", "gg", "gr", "hq", "hf", "hi", "hg", "mq", "mk", "mv", "mi", "mf", "mo")
_SRC = {}
_acc = 0
for _n, _s in zip(_NAMES, _SPLITS):
    _SRC[_n] = _acc
    _acc += _s
N_IN = _acc

SLOT_HEADS = (("g", 0), ("g", 1), ("g", 2), ("g", 3),
              ("g", 4), ("g", 5), ("h", 0), ("m", 0),
              ("h", 1), ("h", 2), ("h", 3), ("h", 4),
              ("m", 1), ("m", 2), ("m", 3), ("m", 4))
_CAT_OFF = {"g": 0, "h": GV, "m": GV + HV}
CAT_PERM = np.concatenate([_CAT_OFF[k] + SLOT * h + np.arange(SLOT) for k, h in SLOT_HEADS])
ML_LANES = np.concatenate([np.full(SLOT, k == "m") for k, _ in SLOT_HEADS])

GH_TILES = (
    dict(q0=0, qw=128, vt=0, heads=tuple((32 * r, GLA_DK, r, "g", r) for r in range(4))),
    dict(q0=128, qw=128, vt=1, heads=((0, GLA_DK, 0, "g", 4), (32, GLA_DK, 1, "g", 5), (64, HG_DK, 2, "h", 0))),
    dict(q0=256, qw=256, vt=2, heads=tuple((64 * r, HG_DK, r, "h", r + 1) for r in range(4))),
)
ML_GROUPS = (
    dict(q0=0, qw=256, vt=3, heads=tuple((64 * r, r, r + 1) for r in range(4))),
    dict(q0=256, qw=128, vt=1, heads=((0, 3, 0),)),
)


def _proj_segments():
    segs = [(_SRC["gq"], GQ), (_SRC["hq"], HQ), (_SRC["gk"], GQ), (_SRC["hf"], HQ)]
    for names in (("gv", "hi", "mv"), ("gr", "hg", "mo")):
        src = dict(zip("ghm", names))
        for kind, h in SLOT_HEADS:
            segs.append((_SRC[src[kind]] + SLOT * h, SLOT))
    pad = MG - MQW - GLA_RANK - ML_H
    for name, gate in (("mq", "mi"), ("mk", "mf")):
        for h in ML_ORDER:
            segs.append((_SRC[name] + ML_DK * h, ML_DK))
        segs.append((_SRC["gg"], GLA_RANK) if name == "mq" else (None, GLA_RANK))
        segs.append((_SRC[gate], ML_H))
        segs.append((None, pad))
    merged = []
    for s, n in segs:
        if merged and s is not None and merged[-1][0] is not None and merged[-1][0] + merged[-1][1] == s:
            merged[-1] = (merged[-1][0], merged[-1][1] + n)
        else:
            merged.append((s, n))
    assert sum(n for _, n in merged) == N_PROJ
    return merged


_PROJ_SEGS = _proj_segments()
_PROJ_IDX = np.concatenate([np.arange(s, s + n) if s is not None else np.zeros(n, np.int64)
                            for s, n in _PROJ_SEGS]).astype(np.int32)
_PROJ_MSK = np.concatenate([np.full(n, 0.0 if s is None else 1.0, np.float32) for s, n in _PROJ_SEGS])


def _sigmoid(x):
    return 1.0 / (1.0 + jnp.exp(-x))


def _log_sigmoid(x):
    return jnp.minimum(x, 0.0) - jnp.log(1.0 + jnp.exp(-jnp.abs(x)))


def _rms(x, g):
    return x * lax.rsqrt(jnp.mean(x * x, axis=-1, keepdims=True) + EPS) * g


def _split3(x):
    hi = x.astype(BF16)
    r1 = x - hi.astype(F32)
    mid = r1.astype(BF16)
    lo = (r1 - mid.astype(F32)).astype(BF16)
    return hi, mid, lo


def _in_range(idx, lo, width):
    return (idx >= lo) & (idx < lo + width)


def _dot(a, b, precision=None):
    return jnp.dot(a, b, preferred_element_type=F32, precision=precision)


def _dot_nt(a, b):
    return lax.dot_general(a, b, (((1,), (1,)), ((), ())), preferred_element_type=F32)


def _dot_tn(a, b, precision=None):
    return lax.dot_general(a, b, (((0,), (0,)), ((), ())), preferred_element_type=F32,
                           precision=precision)


def _mixer_kernel(x_ref, win_ref, wg_ref, wout_ref, seg_ref, vd_ref, v512_ref, cp_ref,
                  s0_in, s1_in, s2_in, ca_in, cb_in, nml_in, mml_in, cbuf_in,
                  y_ref, s0_ref, s1_ref, s2_ref, ca_ref, cb_ref, nml_ref, mml_ref, cbuf_ref,
                  p_ref, q_ref, k_ref, la_ref, mq_ref, mk_ref, ig_ref, lf_ref, xp_ref, cat_ref,
                  *, bb, tb, chunk):
    t = pl.program_id(1)
    rows = bb * tb
    nb = chunk // SUB
    state_pairs = ((s0_in, s0_ref), (s1_in, s1_ref), (s2_in, s2_ref), (ca_in, ca_ref),
                   (cb_in, cb_ref), (nml_in, nml_ref), (mml_in, mml_ref), (cbuf_in, cbuf_ref))

    @pl.when(t == 0)
    def _():
        for src, dst in state_pairs:
            dst[...] = src[...]

    x = x_ref[...].reshape(rows, D_MODEL)
    xn = _rms(x, vd_ref[0:1, :]).astype(BF16)
    p_ref[...] = _dot(xn, win_ref[...])

    half = GH_W // 2
    small_q = p_ref[:, C_MQ + SMALL_OFF:C_MQ + MG]
    z = _dot(small_q.astype(BF16), wg_ref[...]) + v512_ref[0:1, 0:half]
    la_gla = _log_sigmoid(z) * (1.0 / GLA_GATE_NORM)
    is_gla = lax.broadcasted_iota(jnp.int32, (1, half), 1) < GQ
    for c0 in (0, half):
        qraw = p_ref[:, C_Q + c0:C_Q + c0 + half]
        fr = p_ref[:, C_K + c0:C_K + c0 + half]
        lb = v512_ref[1:2, c0:c0 + half]
        q_h = qraw * _sigmoid(qraw) * (HG_DK ** -0.5)
        la_h = jnp.log(lb + (1.0 - lb) * _sigmoid(fr))
        k_h = (1.0 - lb) * _sigmoid(-fr)
        if c0 == 0:
            q_ref[:, 0:half] = jnp.where(is_gla, qraw * (GLA_DK ** -0.5), q_h)
            k_ref[:, 0:half] = jnp.where(is_gla, fr, k_h)
            la_ref[:, 0:half] = jnp.where(is_gla, la_gla, la_h)
        else:
            q_ref[:, half:] = q_h
            k_ref[:, half:] = k_h
            la_ref[:, half:] = la_h

    pre = SUBLANES
    xp_ref[:, 0:pre, :] = cbuf_ref[...]
    xp_ref[:, pre:pre + tb, :] = p_ref[:, C_MQ:C_MQ + 2 * MG].reshape(bb, tb, 2 * MG)
    conv = cp_ref[ML_CONV:ML_CONV + 1, :].reshape(1, 1, 2 * MG)
    for j in range(ML_CONV):
        off = pre - (ML_CONV - 1) + j
        conv = conv + xp_ref[:, off:off + tb, :] * cp_ref[j:j + 1, :].reshape(1, 1, 2 * MG)
    cbuf_ref[...] = xp_ref[:, tb:tb + pre, :]
    qk = (conv * _sigmoid(conv)).reshape(rows, 2 * MG)
    mq_ref[...] = qk[:, 0:MG]
    mk_ref[...] = qk[:, MG:] * (ML_DK ** -0.5)
    ig_ref[...] = small_q + cp_ref[ML_CONV + 1:ML_CONV + 2, 0:LANES]
    lf_ref[...] = _log_sigmoid(p_ref[:, C_MK + SMALL_OFF:C_MK + MG] + cp_ref[ML_CONV + 1:ML_CONV + 2, LANES:2 * LANES])

    sub_sh = SUB.bit_length() - 1
    chunk_sh = chunk.bit_length() - 1
    assert (1 << sub_sh) == SUB and (1 << chunk_sh) == chunk
    ri = lax.broadcasted_iota(jnp.int32, (chunk, chunk), 0)
    ci = lax.broadcasted_iota(jnp.int32, (chunk, chunk), 1)
    causal = ci <= ri
    tri = causal.astype(BF16)
    tri_blk = (causal & ((ri >> sub_sh) == (ci >> sub_sh))).astype(BF16)
    tri3 = jnp.concatenate([tri] * 3, axis=1)
    cum3 = jnp.concatenate([tri3, jnp.concatenate([tri_blk] * 3, axis=1)], axis=0)
    li = lax.broadcasted_iota(jnp.int32, (LANES, 3 * LANES), 0)
    lj = lax.broadcasted_iota(jnp.int32, (LANES, 3 * LANES), 1) & (LANES - 1)
    eye3 = (li == lj).astype(BF16)
    max_heads = TILE // SLOT
    sr = lax.broadcasted_iota(jnp.int32, (max_heads * chunk, nb * chunk), 0) & (chunk - 1)
    sc = lax.broadcasted_iota(jnp.int32, (max_heads * chunk, nb * chunk), 1)
    slab, col = sc >> chunk_sh, sc & (chunk - 1)
    off_ok = (slab < nb - 1) & ((sr >> sub_sh) == slab + 1) & (col < (slab + 1) * SUB)
    diag_ok = (slab == nb - 1) & ((sr >> sub_sh) == (col >> sub_sh)) & (col <= sr)
    row_id = lax.broadcasted_iota(jnp.int32, (chunk, 1), 0)
    sub_pos = row_id & (SUB - 1)
    sub_last = sub_pos == SUB - 1
    lane_v = lax.broadcasted_iota(jnp.int32, (1, TILE), 1)
    lane_m = lax.broadcasted_iota(jnp.int32, (1, LANES), 1)
    gh_refs = (s0_ref, s1_ref, s2_ref)
    ml_refs = (ca_ref, cb_ref)
    chunks_per_stream = tb // chunk

    def chunk_body(ic, carry):
        b_i = ic // chunks_per_stream
        rs = pl.ds(pl.multiple_of(ic * chunk, chunk), chunk)
        v_all = p_ref[rs, C_V:C_V + D_MIX].astype(BF16)
        out_tiles = [None] * (D_MIX // TILE)

        def add_out(vt, val):
            out_tiles[vt] = val if out_tiles[vt] is None else out_tiles[vt] + val

        la = la_ref[rs, :]
        cums = _dot(cum3, jnp.concatenate(_split3(la), axis=0))
        b = cums[0:chunk]
        bq = cums[chunk:]
        q = q_ref[rs, :]
        k = k_ref[rs, :]
        b_last = b[chunk - 1:chunk, :]
        q_in = (q * jnp.exp(bq)).astype(BF16)
        q_st = (q * jnp.exp(b)).astype(BF16)
        k_st = (k * jnp.exp(b_last - b)).astype(BF16)
        dec = jnp.exp(b_last)
        pieces = []
        for i in range(1, nb):
            r_i = b[i * SUB - 1:i * SUB, :]
            kt = k * jnp.exp(jnp.minimum(r_i - b, 0.0))
            pieces.append(jnp.where(row_id < i * SUB, kt, 0.0))
        pieces.append(k * jnp.exp(jnp.minimum(-bq, EXP_CLAMP)))
        worst = jnp.max(jnp.where(sub_last, -bq, 0.0))
        score_sel = off_ok | (diag_ok & (worst <= EXP_CLAMP))
        k_stack = jnp.concatenate(pieces, axis=0).astype(BF16)
        v_rep = jnp.concatenate([v_all] * nb, axis=0)

        for tile, s_ref in zip(GH_TILES, gh_refs):
            q0, qw, vt, heads = tile["q0"], tile["qw"], tile["vt"], tile["heads"]
            nr = len(heads)
            qs = slice(q0, q0 + qw)
            vs = slice(vt * TILE, (vt + 1) * TILE)
            lane_q = lax.broadcasted_iota(jnp.int32, (1, qw), 1)
            row_s = lax.broadcasted_iota(jnp.int32, (TILE, 1), 0)
            qmasks = [_in_range(lane_q, qoff, dk) for qoff, dk, _, _, _ in heads]
            lhs_in = jnp.concatenate([jnp.where(m, q_in[:, qs], 0) for m in qmasks], axis=0)
            lhs_st = jnp.concatenate([jnp.where(m, q_st[:, qs], 0) for m in qmasks], axis=0)
            scores = _dot_nt(lhs_in, k_stack[:, qs])
            scores = jnp.where(score_sel[0:nr * chunk], scores, 0.0).astype(BF16)
            s_t = s_ref[b_i]
            pv = _dot(scores, v_rep[:, vs]) + _dot_nt(lhs_st, s_t.astype(BF16))
            block = None
            for r, (qoff, dk, slot, _, _) in enumerate(heads):
                add_out(vt, jnp.where(_in_range(lane_v, slot * SLOT, SLOT), pv[r * chunk:(r + 1) * chunk], 0.0))
                m = _in_range(row_s, slot * SLOT, SLOT) & qmasks[r]
                block = m if block is None else block | m
            upd = _dot_tn(v_all[:, vs], k_st[:, qs])
            s_ref[b_i] = s_t * dec[:, qs] + jnp.where(block, upd, 0.0)

        lf = lf_ref[rs, :]
        ig = ig_ref[rs, :]
        bm = _dot(tri3, jnp.concatenate(_split3(lf), axis=0))
        u_t = _dot_nt(eye3, jnp.concatenate(_split3(ig - bm), axis=1))
        m_all = mml_ref[b_i]
        m_new = m_all
        for grp, c_ref in zip(ML_GROUPS, ml_refs):
            q0, qw, vt, heads = grp["q0"], grp["qw"], grp["vt"], grp["heads"]
            qs = slice(q0, q0 + qw)
            vs = slice(vt * TILE, (vt + 1) * TILE)
            lane_q = lax.broadcasted_iota(jnp.int32, (1, qw), 1)
            row_q = lax.broadcasted_iota(jnp.int32, (qw, 1), 0)
            mq = mq_ref[rs, qs]
            mk = mk_ref[rs, qs]
            n_row = nml_ref[b_i, :, qs]
            lhs, w_st, w0_st, mi_st = [], [], [], []
            wk_l = jnp.zeros((chunk, qw), F32)
            dc_q = jnp.zeros((1, qw), F32)
            dc_v = jnp.zeros((1, TILE), F32)
            block = None
            for qoff, slot, h in heads:
                qm = _in_range(lane_q, qoff, ML_DK)
                vm = _in_range(lane_v, slot * SLOT, SLOT)
                gl = GATE_LOC + h
                bcol = bm[:, gl:gl + 1]
                m_prev = m_all[:, h:h + 1]
                dlog = jnp.where(causal, bcol + u_t[gl:gl + 1, :], NEG_BIG)
                g = bcol + m_prev
                m_i = jnp.maximum(g, jnp.max(dlog, axis=1, keepdims=True))
                w_st.append(jnp.where(causal, jnp.exp(dlog - m_i), 0.0))
                w0_st.append(jnp.exp(g - m_i))
                mi_st.append(m_i)
                lhs.append(jnp.where(qm, mq, 0.0))
                m_last = m_i[chunk - 1:chunk, :]
                bl = bcol[chunk - 1:chunk, :]
                wk = jnp.exp(bl - bcol + ig[:, gl:gl + 1] - m_last)
                dc = jnp.exp(bl + m_prev - m_last)
                wk_l = jnp.where(qm, wk, wk_l)
                dc_q = jnp.where(qm, dc, dc_q)
                dc_v = jnp.where(vm, dc, dc_v)
                m_new = jnp.where(lane_m == h, m_last, m_new)
                bm_ = _in_range(row_q, qoff, ML_DK) & vm
                block = bm_ if block is None else block | bm_
            lhs = jnp.concatenate(lhs, axis=0)
            w_st = jnp.concatenate(w_st, axis=0)
            w0_st = jnp.concatenate(w0_st, axis=0)
            mi_st = jnp.concatenate(mi_st, axis=0)
            lhs_b = lhs.astype(BF16)
            s = _dot_nt(lhs_b, mk.astype(BF16)) * w_st
            c_st = c_ref[b_i]
            num = _dot(s.astype(BF16), v_all[:, vs]) + w0_st * _dot(lhs_b, c_st.astype(BF16))
            den = (jnp.sum(s, axis=1, keepdims=True)
                   + w0_st * jnp.sum(lhs * n_row, axis=1, keepdims=True))
            hout = num / jnp.maximum(jnp.abs(den), jnp.exp(-mi_st))
            for r, (qoff, slot, h) in enumerate(heads):
                add_out(vt, jnp.where(_in_range(lane_v, slot * SLOT, SLOT), hout[r * chunk:(r + 1) * chunk], 0.0))
            kw = mk * wk_l
            c_ref[b_i] = c_st * dc_v + jnp.where(block, _dot_tn(kw.astype(BF16), v_all[:, vs]), 0.0)
            nml_ref[b_i, :, qs] = n_row * dc_q + jnp.sum(kw, axis=0, keepdims=True)
        mml_ref[b_i] = m_new

        for vt, val in enumerate(out_tiles):
            cat_ref[rs, vt * TILE:(vt + 1) * TILE] = val

        @pl.when(worst > EXP_CLAMP)
        def _():
            hl = lax.broadcasted_iota(jnp.int32, (3 * GH_W, LANES), 0) & (GH_W - 1)
            hc = lax.broadcasted_iota(jnp.int32, (3 * GH_W, LANES), 1)
            head_of_lane = jnp.where(hl < GQ, hl >> (GLA_DK.bit_length() - 1),
                                     GLA_H + ((hl - GQ) >> (HG_DK.bit_length() - 1)))
            seg_q3 = (head_of_lane == hc).astype(BF16)
            bands = []
            for s in range(SUB):
                k_s = k if s == 0 else pltpu.roll(k, s, 0)
                b_s = b if s == 0 else pltpu.roll(b, s, 0)
                pair = q * k_s * jnp.exp(jnp.minimum(b - b_s, 0.0))
                pair = jnp.where(sub_pos >= s, pair, 0.0)
                bands.append(_dot(jnp.concatenate(_split3(pair), axis=1), seg_q3))
            for tile in GH_TILES:
                vs = slice(tile["vt"] * TILE, (tile["vt"] + 1) * TILE)
                corr = None
                for qoff, dk, slot, kind, h in tile["heads"]:
                    hcol = h if kind == "g" else GLA_H + h
                    d = jnp.zeros((chunk, chunk), F32)
                    for s in range(SUB):
                        d = jnp.where(ci == ri - s, bands[s][:, hcol:hcol + 1], d)
                    o = _dot(d.astype(BF16), v_all[:, vs])
                    o = jnp.where(_in_range(lane_v, slot * SLOT, SLOT), o, 0.0)
                    corr = o if corr is None else corr + o
                cat_ref[rs, vs] += corr

        return carry

    lax.fori_loop(0, rows // chunk, chunk_body, 0)

    o = cat_ref[...]
    ms = _dot((o * o).astype(BF16), seg_ref[...])
    og = p_ref[:, C_OG:C_OG + D_MIX]
    gate = _sigmoid(og) * jnp.where(vd_ref[3:4, :] > 0.5, 1.0, og)
    cat = (o * lax.rsqrt(ms + EPS) * vd_ref[2:3, :] * gate).astype(BF16)
    hm = _dot(cat, wout_ref[...])
    y = x + _rms(hm, vd_ref[1:2, :])
    y_ref[...] = y.reshape(bb, tb, D_MODEL)


def _const_spec(shape):
    nd = len(shape)
    return pl.BlockSpec(shape, lambda b, t: (0,) * nd, pipeline_mode=pl.Buffered(1))


def _state_spec(shape, bb):
    nd = len(shape)
    return pl.BlockSpec((bb,) + tuple(shape[1:]), lambda b, t: (b,) + (0,) * (nd - 1))


_MIXER_STATES = ("s0", "s1", "s2", "ca", "cb", "nml", "mml", "cbuf")


def _mixer_call(x, lw, st, *, bb, tb):
    B, T, _ = x.shape
    chunk = min(MAX_CHUNK, tb)
    rows = bb * tb
    grid = (B // bb, T // tb)
    x_spec = pl.BlockSpec((bb, tb, D_MODEL), lambda b, t: (b, t, 0))
    consts = (lw["w_in"], lw["w_gate"], lw["w_out"], lw["seg"], lw["vec_d"], lw["vec_512"], lw["conv_p"])
    states = tuple(st[n] for n in _MIXER_STATES)
    in_specs = ([x_spec] + [_const_spec(c.shape) for c in consts]
                + [_state_spec(s.shape, bb) for s in states])
    out_shape = ([jax.ShapeDtypeStruct(x.shape, F32)]
                 + [jax.ShapeDtypeStruct(s.shape, F32) for s in states])
    out_specs = [x_spec] + [_state_spec(s.shape, bb) for s in states]
    scratch = [
        pltpu.VMEM((rows, N_PROJ), F32),
        pltpu.VMEM((rows, GH_W), F32),
        pltpu.VMEM((rows, GH_W), F32),
        pltpu.VMEM((rows, GH_W), F32),
        pltpu.VMEM((rows, MG), F32),
        pltpu.VMEM((rows, MG), F32),
        pltpu.VMEM((rows, LANES), F32),
        pltpu.VMEM((rows, LANES), F32),
        pltpu.VMEM((bb, tb + SUBLANES, 2 * MG), F32),
        pltpu.VMEM((rows, D_MIX), F32),
    ]
    outs = pl.pallas_call(
        functools.partial(_mixer_kernel, bb=bb, tb=tb, chunk=chunk),
        grid=grid, in_specs=in_specs, out_specs=out_specs, out_shape=out_shape,
        scratch_shapes=scratch,
        compiler_params=pltpu.CompilerParams(
            dimension_semantics=("arbitrary", "arbitrary"), vmem_limit_bytes=VMEM_LIMIT),
        name="mixer",
    )(x, *consts, *states)
    return outs[0], dict(zip(_MIXER_STATES, outs[1:]))


def _ffn_kernel(x_ref, wup_ref, wdn_ref, vd_ref, cp_ref, fbuf_in, y_ref, fbuf_ref, xp_ref,
                *, bb, tb):
    t = pl.program_id(1)
    rows = bb * tb
    pre = SUBLANES

    @pl.when(t == 0)
    def _():
        fbuf_ref[...] = fbuf_in[...]

    x = x_ref[...].reshape(rows, D_MODEL)
    xn = _rms(x, vd_ref[0:1, :]).astype(BF16)
    up = _dot(xn, wup_ref[...])
    xp_ref[:, 0:pre, :] = fbuf_ref[...]
    xp_ref[:, pre:pre + tb, :] = up.reshape(bb, tb, 2 * D_FF)
    conv = cp_ref[FFN_CONV:FFN_CONV + 1, :].reshape(1, 1, 2 * D_FF)
    for j in range(FFN_CONV):
        off = pre - (FFN_CONV - 1) + j
        conv = conv + xp_ref[:, off:off + tb, :] * cp_ref[j:j + 1, :].reshape(1, 1, 2 * D_FF)
    fbuf_ref[...] = xp_ref[:, tb:tb + pre, :]
    conv = conv.reshape(rows, 2 * D_FF)
    gate = conv[:, 0:D_FF]
    val = conv[:, D_FF:]
    c0 = 0.7978845608028654
    gelu = 0.5 * gate * (1.0 + jnp.tanh(c0 * (gate + 0.044715 * gate * gate * gate)))
    hmid = (gelu * val).astype(BF16)
    hd = _dot(hmid, wdn_ref[...])
    y = x + _rms(hd, vd_ref[1:2, :])
    y_ref[...] = y.reshape(bb, tb, D_MODEL)


def _ffn_call(x, lw, fbuf, *, bb, tb):
    B, T, _ = x.shape
    grid = (B // bb, T // tb)
    x_spec = pl.BlockSpec((bb, tb, D_MODEL), lambda b, t: (b, t, 0))
    consts = (lw["w_up"], lw["w_down"], lw["vec_ffn"], lw["ffn_conv_p"])
    in_specs = [x_spec] + [_const_spec(c.shape) for c in consts] + [_state_spec(fbuf.shape, bb)]
    out_shape = [jax.ShapeDtypeStruct(x.shape, F32), jax.ShapeDtypeStruct(fbuf.shape, F32)]
    out_specs = [x_spec, _state_spec(fbuf.shape, bb)]
    y, fnew = pl.pallas_call(
        functools.partial(_ffn_kernel, bb=bb, tb=tb),
        grid=grid, in_specs=in_specs, out_specs=out_specs, out_shape=out_shape,
        scratch_shapes=[pltpu.VMEM((bb, tb + SUBLANES, 2 * D_FF), F32)],
        compiler_params=pltpu.CompilerParams(
            dimension_semantics=("arbitrary", "arbitrary"), vmem_limit_bytes=VMEM_LIMIT),
        name="ffn",
    )(x, *consts, fbuf)
    return y, fnew


def _pad_rows(a, n=SUBLANES):
    return jnp.pad(a, ((0, n - a.shape[0]), (0, 0)))


def _mqk(a):
    lead = a.shape[:-1]
    parts = []
    for half in (a[..., :MQW], a[..., MQW:]):
        hh = half.reshape(lead + (ML_H, ML_DK))
        parts += [hh[..., h, :] for h in ML_ORDER] + [jnp.zeros(lead + (MG - MQW,), a.dtype)]
    return jnp.concatenate(parts, axis=-1)


def _mqk_inv(a):
    inv = np.argsort(np.asarray(ML_ORDER))
    parts = []
    for base in (0, MG):
        parts += [a[..., base + ML_DK * int(p):base + ML_DK * (int(p) + 1)] for p in inv]
    return jnp.concatenate(parts, axis=-1)


def _to_slots(a, axis):
    parts = [lax.slice_in_dim(a, _CAT_OFF[k] + SLOT * h, _CAT_OFF[k] + SLOT * (h + 1), axis=axis)
             for k, h in SLOT_HEADS]
    return jnp.concatenate(parts, axis=axis)


def _layer_weights(l, lb_all, g_mix_pre, g_mix_post, g_ffn_pre, g_ffn_post, w_in, gla_w_gate,
                   gla_b_gate, ml_conv_w, ml_conv_b, ml_b_i, ml_b_f, g_head, w_out, ffn_w_up,
                   ffn_conv_w, ffn_conv_b, ffn_w_down):
    w = w_in[l]
    w_in_p = (jnp.take(w, jnp.asarray(_PROJ_IDX), axis=1) * jnp.asarray(_PROJ_MSK)).astype(BF16)
    half = GH_W // 2
    w_gate = jnp.zeros((LANES, half), F32).at[GG_LOC:GG_LOC + GLA_RANK, 0:GQ].set(gla_w_gate[l]).astype(BF16)
    head_id = np.arange(D_MIX) // SLOT
    seg = jnp.asarray((head_id[:, None] == head_id[None, :]).astype(np.float32) / SLOT, dtype=BF16)
    vec_d = _pad_rows(jnp.stack([g_mix_pre[l], g_mix_post[l], _to_slots(g_head[l], 0),
                                 jnp.asarray(ML_LANES, F32)]))
    vec_512 = _pad_rows(jnp.stack([jnp.pad(gla_b_gate[l], (0, HQ)), jnp.pad(lb_all[l], (GQ, 0))]))
    gate_b = jnp.zeros((2 * MG,), F32)
    gate_b = gate_b.at[GATE_LOC:GATE_LOC + ML_H].set(ml_b_i[l])
    gate_b = gate_b.at[LANES + GATE_LOC:LANES + GATE_LOC + ML_H].set(ml_b_f[l])
    conv_p = _pad_rows(jnp.concatenate([_mqk(ml_conv_w[l]), _mqk(ml_conv_b[l])[None], gate_b[None]], axis=0))
    return dict(
        w_in=w_in_p, w_gate=w_gate, w_out=jnp.take(w_out[l], jnp.asarray(CAT_PERM, jnp.int32), axis=0).astype(BF16), seg=seg, vec_d=vec_d,
        vec_512=vec_512, conv_p=conv_p,
        w_up=ffn_w_up[l].astype(BF16), w_down=ffn_w_down[l].astype(BF16),
        vec_ffn=_pad_rows(jnp.stack([g_ffn_pre[l], g_ffn_post[l]])),
        ffn_conv_p=_pad_rows(jnp.concatenate([ffn_conv_w[l], ffn_conv_b[l][None]], axis=0)),
    )


def _place(block, r0, c0, shape):
    return jnp.pad(block, ((0, 0), (r0, shape[0] - r0 - block.shape[1]), (c0, shape[1] - c0 - block.shape[2])))


def _pack_states(s_gla, s_hg, c_ml, n_ml, m_ml, buf_ml, buf_ffn):
    B = s_gla.shape[0]
    st = {}
    for i, tile in enumerate(GH_TILES):
        acc = 0.0
        for qoff, dk, slot, kind, h in tile["heads"]:
            s = s_gla[:, h] if kind == "g" else s_hg[:, h]
            acc = acc + _place(jnp.swapaxes(s, 1, 2), slot * SLOT, qoff, (TILE, tile["qw"]))
        st["s%d" % i] = acc
    for name, grp in zip(("ca", "cb"), ML_GROUPS):
        acc = 0.0
        for qoff, slot, h in grp["heads"]:
            acc = acc + _place(c_ml[:, h], qoff, slot * SLOT, (grp["qw"], TILE))
        st[name] = acc
    n_perm = jnp.concatenate([n_ml[:, h] for h in ML_ORDER], axis=-1)
    st["nml"] = jnp.pad(n_perm, ((0, 0), (0, MG - MQW)))[:, None, :]
    st["mml"] = jnp.pad(m_ml, ((0, 0), (0, LANES - ML_H)))[:, None, :]
    st["cbuf"] = jnp.pad(_mqk(buf_ml), ((0, 0), (SUBLANES - (ML_CONV - 1), 0), (0, 0)))
    fbuf = jnp.pad(buf_ffn, ((0, 0), (SUBLANES - (FFN_CONV - 1), 0), (0, 0)))
    return st, fbuf


def _zero_states(B):
    st = {"s%d" % i: jnp.zeros((B, TILE, tile["qw"]), F32) for i, tile in enumerate(GH_TILES)}
    for name, grp in zip(("ca", "cb"), ML_GROUPS):
        st[name] = jnp.zeros((B, grp["qw"], TILE), F32)
    st["nml"] = jnp.zeros((B, 1, MG), F32)
    st["mml"] = jnp.zeros((B, 1, LANES), F32)
    st["cbuf"] = jnp.zeros((B, SUBLANES, 2 * MG), F32)
    return st, jnp.zeros((B, SUBLANES, 2 * D_FF), F32)


def _unpack_states(st, fbuf):
    gl, hg, cm = {}, {}, {}
    for i, tile in enumerate(GH_TILES):
        s_t = st["s%d" % i]
        for qoff, dk, slot, kind, h in tile["heads"]:
            s = jnp.swapaxes(s_t[:, slot * SLOT:(slot + 1) * SLOT, qoff:qoff + dk], 1, 2)
            (gl if kind == "g" else hg)[h] = s
    for name, grp in zip(("ca", "cb"), ML_GROUPS):
        for qoff, slot, h in grp["heads"]:
            cm[h] = st[name][:, qoff:qoff + ML_DK, slot * SLOT:(slot + 1) * SLOT]
    n = st["nml"][:, 0, :]
    pos = {h: i for i, h in enumerate(ML_ORDER)}
    n_ml = jnp.stack([n[:, ML_DK * pos[h]:ML_DK * (pos[h] + 1)] for h in range(ML_H)], axis=1)
    m_ml = st["mml"][:, 0, :ML_H]
    buf_ml = _mqk_inv(st["cbuf"][:, SUBLANES - (ML_CONV - 1):, :])
    buf_ffn = fbuf[:, SUBLANES - (FFN_CONV - 1):, :]
    return (jnp.stack([gl[h] for h in range(GLA_H)], 1), jnp.stack([hg[h] for h in range(HG_H)], 1),
            jnp.stack([cm[h] for h in range(ML_H)], 1), n_ml, m_ml, buf_ml, buf_ffn)


def _block_sizes(B, T):
    if T % 512 == 0:
        return 1, 512, 256
    if T % 256 == 0:
        return 1, 256, 256
    bb = 4 if B % 4 == 0 else 1
    return bb, T, T


def _run_trunk(x, states, weights):
    B, T, _ = x.shape
    bb, tb_mix, tb_ffn = _block_sizes(B, T)
    new = [[] for _ in range(7)]
    for l in range(DEPTH):
        if states is None:
            st, fbuf = _zero_states(B)
        else:
            st, fbuf = _pack_states(*(s[l] for s in states))
        x, st = _mixer_call(x, weights[l], st, bb=bb, tb=tb_mix)
        x, fbuf = _ffn_call(x, weights[l], fbuf, bb=bb, tb=tb_ffn)
        for i, s in enumerate(_unpack_states(st, fbuf)):
            new[i].append(s)
    return x, [jnp.stack(s, axis=0) for s in new]


def kernel(x_prompt, x_sample, state_gla, state_hgrn, state_mlstm_C, state_mlstm_n,
           state_mlstm_m, cache_mlstm_conv, cache_ffn_conv, g_mix_pre, g_mix_post, g_ffn_pre,
           g_ffn_post, w_in, gla_w_gate, gla_b_gate, hgrn_lb, ml_conv_w, ml_conv_b, ml_b_i,
           ml_b_f, g_head, w_out, ffn_w_up, ffn_conv_w, ffn_conv_b, ffn_w_down):
    sm = jax.nn.softmax(hgrn_lb.astype(F32), axis=0)
    lb_all = jnp.cumsum(sm, axis=0) - sm[0:1]
    weights = [
        _layer_weights(l, lb_all, g_mix_pre, g_mix_post, g_ffn_pre, g_ffn_post, w_in, gla_w_gate,
                       gla_b_gate, ml_conv_w, ml_conv_b, ml_b_i, ml_b_f, g_head, w_out, ffn_w_up,
                       ffn_conv_w, ffn_conv_b, ffn_w_down)
        for l in range(DEPTH)]
    y_prompt, ps = _run_trunk(x_prompt, None, weights)
    sample_states = (state_gla, state_hgrn, state_mlstm_C, state_mlstm_n, state_mlstm_m,
                     cache_mlstm_conv, cache_ffn_conv)
    y_sample, ss = _run_trunk(x_sample, sample_states, weights)
    return (y_prompt, y_sample, *ps, *ss)
```

```python
import functools

import numpy as np
import jax
import jax.numpy as jnp
from jax import lax
from jax.experimental import pallas as pl
from jax.experimental.pallas import tpu as pltpu

F32 = jnp.float32
BF16 = jnp.bfloat16
HIGHEST = lax.Precision.HIGHEST

D_MODEL = 1024
DEPTH = 2
GLA_H, GLA_DK, GLA_DV, GLA_RANK, GLA_GATE_NORM = 6, 32, 64, 16, 16.0
HG_H, HG_DK, HG_DV = 5, 64, 64
ML_H, ML_DK, ML_DV, ML_CONV = 5, 64, 64, 4
D_MIX = GLA_H * GLA_DV + HG_H * HG_DV + ML_H * ML_DV
D_FF = 2816
FFN_CONV = 3
EPS = 1e-6
NEG_BIG = -1e30

LANES = 128
SUBLANES = 8
TILE = 256
SLOT = 64
SUB = 16
MAX_CHUNK = 64
EXP_CLAMP = 80.0
VMEM_LIMIT = 56 * 1024 * 1024

GQ, HQ = GLA_H * GLA_DK, HG_H * HG_DK
GV, HV, MV = GLA_H * GLA_DV, HG_H * HG_DV, ML_H * ML_DV
MQW = ML_H * ML_DK
GH_W = GQ + HQ
MG = 384
C_Q = 0
C_K = C_Q + GH_W
C_V = C_K + GH_W
C_OG = C_V + D_MIX
C_MQ = C_OG + D_MIX
C_MK = C_MQ + MG
N_PROJ = C_MK + MG
SMALL_OFF = 256
GG_LOC = MQW - SMALL_OFF
GATE_LOC = GG_LOC + GLA_RANK
ML_ORDER = (1, 2, 3, 4, 0)

_SPLITS = (GQ, GQ, GV, GLA_RANK, GV, HQ, HQ, HV, HV, MQW, MQW, MV, ML_H, ML_H, MV)
_NAMES = ("gq", "gk", "gv", "gg", "gr", "hq", "hf", "hi", "hg", "mq", "mk", "mv", "mi", "mf", "mo")
_SRC = {}
_acc = 0
for _n, _s in zip(_NAMES, _SPLITS):
    _SRC[_n] = _acc
    _acc += _s
N_IN = _acc

SLOT_HEADS = (("g", 0), ("g", 1), ("g", 2), ("g", 3),
              ("g", 4), ("g", 5), ("h", 0), ("m", 0),
              ("h", 1), ("h", 2), ("h", 3), ("h", 4),
              ("m", 1), ("m", 2), ("m", 3), ("m", 4))
_CAT_OFF = {"g": 0, "h": GV, "m": GV + HV}
CAT_PERM = np.concatenate([_CAT_OFF[k] + SLOT * h + np.arange(SLOT) for k, h in SLOT_HEADS])
ML_LANES = np.concatenate([np.full(SLOT, k == "m") for k, _ in SLOT_HEADS])

GH_TILES = (
    dict(q0=0, qw=128, vt=0, heads=tuple((32 * r, GLA_DK, r, "g", r) for r in range(4))),
    dict(q0=128, qw=128, vt=1, heads=((0, GLA_DK, 0, "g", 4), (32, GLA_DK, 1, "g", 5), (64, HG_DK, 2, "h", 0))),
    dict(q0=256, qw=256, vt=2, heads=tuple((64 * r, HG_DK, r, "h", r + 1) for r in range(4))),
)
ML_GROUPS = (
    dict(q0=0, qw=256, vt=3, heads=tuple((64 * r, r, r + 1) for r in range(4))),
    dict(q0=256, qw=128, vt=1, heads=((0, 3, 0),)),
)


def _proj_segments():
    segs = [(_SRC["gq"], GQ), (_SRC["hq"], HQ), (_SRC["gk"], GQ), (_SRC["hf"], HQ)]
    for names in (("gv", "hi", "mv"), ("gr", "hg", "mo")):
        src = dict(zip("ghm", names))
        for kind, h in SLOT_HEADS:
            segs.append((_SRC[src[kind]] + SLOT * h, SLOT))
    pad = MG - MQW - GLA_RANK - ML_H
    for name, gate in (("mq", "mi"), ("mk", "mf")):
        for h in ML_ORDER:
            segs.append((_SRC[name] + ML_DK * h, ML_DK))
        segs.append((_SRC["gg"], GLA_RANK) if name == "mq" else (None, GLA_RANK))
        segs.append((_SRC[gate], ML_H))
        segs.append((None, pad))
    merged = []
    for s, n in segs:
        if merged and s is not None and merged[-1][0] is not None and merged[-1][0] + merged[-1][1] == s:
            merged[-1] = (merged[-1][0], merged[-1][1] + n)
        else:
            merged.append((s, n))
    assert sum(n for _, n in merged) == N_PROJ
    return merged


_PROJ_SEGS = _proj_segments()
_PROJ_IDX = np.concatenate([np.arange(s, s + n) if s is not None else np.zeros(n, np.int64)
                            for s, n in _PROJ_SEGS]).astype(np.int32)
_PROJ_MSK = np.concatenate([np.full(n, 0.0 if s is None else 1.0, np.float32) for s, n in _PROJ_SEGS])


def _sigmoid(x):
    return 1.0 / (1.0 + jnp.exp(-x))


def _log_sigmoid(x):
    return jnp.minimum(x, 0.0) - jnp.log(1.0 + jnp.exp(-jnp.abs(x)))


def _rms(x, g):
    return x * lax.rsqrt(jnp.mean(x * x, axis=-1, keepdims=True) + EPS) * g


def _split3(x):
    hi = x.astype(BF16)
    r1 = x - hi.astype(F32)
    mid = r1.astype(BF16)
    lo = (r1 - mid.astype(F32)).astype(BF16)
    return hi, mid, lo


def _in_range(idx, lo, width):
    return (idx >= lo) & (idx < lo + width)


def _dot(a, b, precision=None):
    return jnp.dot(a, b, preferred_element_type=F32, precision=precision)


def _dot_nt(a, b):
    return lax.dot_general(a, b, (((1,), (1,)), ((), ())), preferred_element_type=F32)


def _dot_tn(a, b, precision=None):
    return lax.dot_general(a, b, (((0,), (0,)), ((), ())), preferred_element_type=F32,
                           precision=precision)


def _mixer_kernel(x_ref, win_ref, wg_ref, wout_ref, seg_ref, vd_ref, v512_ref, cp_ref,
                  s0_in, s1_in, s2_in, ca_in, cb_in, nml_in, mml_in, cbuf_in,
                  y_ref, s0_ref, s1_ref, s2_ref, ca_ref, cb_ref, nml_ref, mml_ref, cbuf_ref,
                  p_ref, q_ref, k_ref, la_ref, mq_ref, mk_ref, ig_ref, lf_ref, xp_ref, cat_ref,
                  *, bb, tb, chunk):
    t = pl.program_id(1)
    rows = bb * tb
    nb = chunk // SUB
    state_pairs = ((s0_in, s0_ref), (s1_in, s1_ref), (s2_in, s2_ref), (ca_in, ca_ref),
                   (cb_in, cb_ref), (nml_in, nml_ref), (mml_in, mml_ref), (cbuf_in, cbuf_ref))

    @pl.when(t == 0)
    def _():
        for src, dst in state_pairs:
            dst[...] = src[...]

    x = x_ref[...].reshape(rows, D_MODEL)
    xn = _rms(x, vd_ref[0:1, :]).astype(BF16)
    p_ref[...] = _dot(xn, win_ref[...])

    half = GH_W // 2
    small_q = p_ref[:, C_MQ + SMALL_OFF:C_MQ + MG]
    z = _dot(small_q.astype(BF16), wg_ref[...]) + v512_ref[0:1, 0:half]
    la_gla = _log_sigmoid(z) * (1.0 / GLA_GATE_NORM)
    is_gla = lax.broadcasted_iota(jnp.int32, (1, half), 1) < GQ
    for c0 in (0, half):
        qraw = p_ref[:, C_Q + c0:C_Q + c0 + half]
        fr = p_ref[:, C_K + c0:C_K + c0 + half]
        lb = v512_ref[1:2, c0:c0 + half]
        q_h = qraw * _sigmoid(qraw) * (HG_DK ** -0.5)
        la_h = jnp.log(lb + (1.0 - lb) * _sigmoid(fr))
        k_h = (1.0 - lb) * _sigmoid(-fr)
        if c0 == 0:
            q_ref[:, 0:half] = jnp.where(is_gla, qraw * (GLA_DK ** -0.5), q_h)
            k_ref[:, 0:half] = jnp.where(is_gla, fr, k_h)
            la_ref[:, 0:half] = jnp.where(is_gla, la_gla, la_h)
        else:
            q_ref[:, half:] = q_h
            k_ref[:, half:] = k_h
            la_ref[:, half:] = la_h

    pre = SUBLANES
    xp_ref[:, 0:pre, :] = cbuf_ref[...]
    xp_ref[:, pre:pre + tb, :] = p_ref[:, C_MQ:C_MQ + 2 * MG].reshape(bb, tb, 2 * MG)
    conv = cp_ref[ML_CONV:ML_CONV + 1, :].reshape(1, 1, 2 * MG)
    for j in range(ML_CONV):
        off = pre - (ML_CONV - 1) + j
        conv = conv + xp_ref[:, off:off + tb, :] * cp_ref[j:j + 1, :].reshape(1, 1, 2 * MG)
    cbuf_ref[...] = xp_ref[:, tb:tb + pre, :]
    qk = (conv * _sigmoid(conv)).reshape(rows, 2 * MG)
    mq_ref[...] = qk[:, 0:MG]
    mk_ref[...] = qk[:, MG:] * (ML_DK ** -0.5)
    ig_ref[...] = small_q + cp_ref[ML_CONV + 1:ML_CONV + 2, 0:LANES]
    lf_ref[...] = _log_sigmoid(p_ref[:, C_MK + SMALL_OFF:C_MK + MG] + cp_ref[ML_CONV + 1:ML_CONV + 2, LANES:2 * LANES])

    sub_sh = SUB.bit_length() - 1
    chunk_sh = chunk.bit_length() - 1
    assert (1 << sub_sh) == SUB and (1 << chunk_sh) == chunk
    ri = lax.broadcasted_iota(jnp.int32, (chunk, chunk), 0)
    ci = lax.broadcasted_iota(jnp.int32, (chunk, chunk), 1)
    causal = ci <= ri
    tri = causal.astype(BF16)
    tri_blk = (causal & ((ri >> sub_sh) == (ci >> sub_sh))).astype(BF16)
    tri3 = jnp.concatenate([tri] * 3, axis=1)
    cum3 = jnp.concatenate([tri3, jnp.concatenate([tri_blk] * 3, axis=1)], axis=0)
    li = lax.broadcasted_iota(jnp.int32, (LANES, 3 * LANES), 0)
    lj = lax.broadcasted_iota(jnp.int32, (LANES, 3 * LANES), 1) & (LANES - 1)
    eye3 = (li == lj).astype(BF16)
    max_heads = TILE // SLOT
    sr = lax.broadcasted_iota(jnp.int32, (max_heads * chunk, nb * chunk), 0) & (chunk - 1)
    sc = lax.broadcasted_iota(jnp.int32, (max_heads * chunk, nb * chunk), 1)
    slab, col = sc >> chunk_sh, sc & (chunk - 1)
    off_ok = (slab < nb - 1) & ((sr >> sub_sh) == slab + 1) & (col < (slab + 1) * SUB)
    diag_ok = (slab == nb - 1) & ((sr >> sub_sh) == (col >> sub_sh)) & (col <= sr)
    row_id = lax.broadcasted_iota(jnp.int32, (chunk, 1), 0)
    sub_pos = row_id & (SUB - 1)
    sub_last = sub_pos == SUB - 1
    lane_v = lax.broadcasted_iota(jnp.int32, (1, TILE), 1)
    lane_m = lax.broadcasted_iota(jnp.int32, (1, LANES), 1)
    gh_refs = (s0_ref, s1_ref, s2_ref)
    ml_refs = (ca_ref, cb_ref)
    chunks_per_stream = tb // chunk

    def chunk_body(ic, carry):
        b_i = ic // chunks_per_stream
        rs = pl.ds(pl.multiple_of(ic * chunk, chunk), chunk)
        v_all = p_ref[rs, C_V:C_V + D_MIX].astype(BF16)
        out_tiles = [None] * (D_MIX // TILE)

        def add_out(vt, val):
            out_tiles[vt] = val if out_tiles[vt] is None else out_tiles[vt] + val

        la = la_ref[rs, :]
        cums = _dot(cum3, jnp.concatenate(_split3(la), axis=0))
        b = cums[0:chunk]
        bq = cums[chunk:]
        q = q_ref[rs, :]
        k = k_ref[rs, :]
        b_last = b[chunk - 1:chunk, :]
        q_in = (q * jnp.exp(bq)).astype(BF16)
        q_st = (q * jnp.exp(b)).astype(BF16)
        k_st = (k * jnp.exp(b_last - b)).astype(BF16)
        dec = jnp.exp(b_last)
        pieces = []
        for i in range(1, nb):
            r_i = b[i * SUB - 1:i * SUB, :]
            kt = k * jnp.exp(jnp.minimum(r_i - b, 0.0))
            pieces.append(jnp.where(row_id < i * SUB, kt, 0.0))
        pieces.append(k * jnp.exp(jnp.minimum(-bq, EXP_CLAMP)))
        worst = jnp.max(jnp.where(sub_last, -bq, 0.0))
        score_sel = off_ok | (diag_ok & (worst <= EXP_CLAMP))
        k_stack = jnp.concatenate(pieces, axis=0).astype(BF16)
        v_rep = jnp.concatenate([v_all] * nb, axis=0)

        for tile, s_ref in zip(GH_TILES, gh_refs):
            q0, qw, vt, heads = tile["q0"], tile["qw"], tile["vt"], tile["heads"]
            nr = len(heads)
            qs = slice(q0, q0 + qw)
            vs = slice(vt * TILE, (vt + 1) * TILE)
            lane_q = lax.broadcasted_iota(jnp.int32, (1, qw), 1)
            row_s = lax.broadcasted_iota(jnp.int32, (TILE, 1), 0)
            qmasks = [_in_range(lane_q, qoff, dk) for qoff, dk, _, _, _ in heads]
            lhs_in = jnp.concatenate([jnp.where(m, q_in[:, qs], 0) for m in qmasks], axis=0)
            lhs_st = jnp.concatenate([jnp.where(m, q_st[:, qs], 0) for m in qmasks], axis=0)
            scores = _dot_nt(lhs_in, k_stack[:, qs])
            scores = jnp.where(score_sel[0:nr * chunk], scores, 0.0).astype(BF16)
            s_t = s_ref[b_i]
            pv = _dot(scores, v_rep[:, vs]) + _dot_nt(lhs_st, s_t.astype(BF16))
            block = None
            for r, (qoff, dk, slot, _, _) in enumerate(heads):
                add_out(vt, jnp.where(_in_range(lane_v, slot * SLOT, SLOT), pv[r * chunk:(r + 1) * chunk], 0.0))
                m = _in_range(row_s, slot * SLOT, SLOT) & qmasks[r]
                block = m if block is None else block | m
            upd = _dot_tn(v_all[:, vs], k_st[:, qs])
            s_ref[b_i] = s_t * dec[:, qs] + jnp.where(block, upd, 0.0)

        lf = lf_ref[rs, :]
        ig = ig_ref[rs, :]
        bm = _dot(tri3, jnp.concatenate(_split3(lf), axis=0))
        u_t = _dot_nt(eye3, jnp.concatenate(_split3(ig - bm), axis=1))
        m_all = mml_ref[b_i]
        m_new = m_all
        for grp, c_ref in zip(ML_GROUPS, ml_refs):
            q0, qw, vt, heads = grp["q0"], grp["qw"], grp["vt"], grp["heads"]
            qs = slice(q0, q0 + qw)
            vs = slice(vt * TILE, (vt + 1) * TILE)
            lane_q = lax.broadcasted_iota(jnp.int32, (1, qw), 1)
            row_q = lax.broadcasted_iota(jnp.int32, (qw, 1), 0)
            mq = mq_ref[rs, qs]
            mk = mk_ref[rs, qs]
            n_row = nml_ref[b_i, :, qs]
            lhs, w_st, w0_st, mi_st = [], [], [], []
            wk_l = jnp.zeros((chunk, qw), F32)
            dc_q = jnp.zeros((1, qw), F32)
            dc_v = jnp.zeros((1, TILE), F32)
            block = None
            for qoff, slot, h in heads:
                qm = _in_range(lane_q, qoff, ML_DK)
                vm = _in_range(lane_v, slot * SLOT, SLOT)
                gl = GATE_LOC + h
                bcol = bm[:, gl:gl + 1]
                m_prev = m_all[:, h:h + 1]
                dlog = jnp.where(causal, bcol + u_t[gl:gl + 1, :], NEG_BIG)
                g = bcol + m_prev
                m_i = jnp.maximum(g, jnp.max(dlog, axis=1, keepdims=True))
                w_st.append(jnp.where(causal, jnp.exp(dlog - m_i), 0.0))
                w0_st.append(jnp.exp(g - m_i))
                mi_st.append(m_i)
                lhs.append(jnp.where(qm, mq, 0.0))
                m_last = m_i[chunk - 1:chunk, :]
                bl = bcol[chunk - 1:chunk, :]
                wk = jnp.exp(bl - bcol + ig[:, gl:gl + 1] - m_last)
                dc = jnp.exp(bl + m_prev - m_last)
                wk_l = jnp.where(qm, wk, wk_l)
                dc_q = jnp.where(qm, dc, dc_q)
                dc_v = jnp.where(vm, dc, dc_v)
                m_new = jnp.where(lane_m == h, m_last, m_new)
                bm_ = _in_range(row_q, qoff, ML_DK) & vm
                block = bm_ if block is None else block | bm_
            lhs = jnp.concatenate(lhs, axis=0)
            w_st = jnp.concatenate(w_st, axis=0)
            w0_st = jnp.concatenate(w0_st, axis=0)
            mi_st = jnp.concatenate(mi_st, axis=0)
            lhs_b = lhs.astype(BF16)
            s = _dot_nt(lhs_b, mk.astype(BF16)) * w_st
            c_st = c_ref[b_i]
            num = _dot(s.astype(BF16), v_all[:, vs]) + w0_st * _dot(lhs_b, c_st.astype(BF16))
            den = (jnp.sum(s, axis=1, keepdims=True)
                   + w0_st * jnp.sum(lhs * n_row, axis=1, keepdims=True))
            hout = num / jnp.maximum(jnp.abs(den), jnp.exp(-mi_st))
            for r, (qoff, slot, h) in enumerate(heads):
                add_out(vt, jnp.where(_in_range(lane_v, slot * SLOT, SLOT), hout[r * chunk:(r + 1) * chunk], 0.0))
            kw = mk * wk_l
            c_ref[b_i] = c_st * dc_v + jnp.where(block, _dot_tn(kw.astype(BF16), v_all[:, vs]), 0.0)
            nml_ref[b_i, :, qs] = n_row * dc_q + jnp.sum(kw, axis=0, keepdims=True)
        mml_ref[b_i] = m_new

        for vt, val in enumerate(out_tiles):
            cat_ref[rs, vt * TILE:(vt + 1) * TILE] = val

        @pl.when(worst > EXP_CLAMP)
        def _():
            hl = lax.broadcasted_iota(jnp.int32, (3 * GH_W, LANES), 0) & (GH_W - 1)
            hc = lax.broadcasted_iota(jnp.int32, (3 * GH_W, LANES), 1)
            head_of_lane = jnp.where(hl < GQ, hl >> (GLA_DK.bit_length() - 1),
                                     GLA_H + ((hl - GQ) >> (HG_DK.bit_length() - 1)))
            seg_q3 = (head_of_lane == hc).astype(BF16)
            bands = []
            for s in range(SUB):
                k_s = k if s == 0 else pltpu.roll(k, s, 0)
                b_s = b if s == 0 else pltpu.roll(b, s, 0)
                pair = q * k_s * jnp.exp(jnp.minimum(b - b_s, 0.0))
                pair = jnp.where(sub_pos >= s, pair, 0.0)
                bands.append(_dot(jnp.concatenate(_split3(pair), axis=1), seg_q3))
            for tile in GH_TILES:
                vs = slice(tile["vt"] * TILE, (tile["vt"] + 1) * TILE)
                corr = None
                for qoff, dk, slot, kind, h in tile["heads"]:
                    hcol = h if kind == "g" else GLA_H + h
                    d = jnp.zeros((chunk, chunk), F32)
                    for s in range(SUB):
                        d = jnp.where(ci == ri - s, bands[s][:, hcol:hcol + 1], d)
                    o = _dot(d.astype(BF16), v_all[:, vs])
                    o = jnp.where(_in_range(lane_v, slot * SLOT, SLOT), o, 0.0)
                    corr = o if corr is None else corr + o
                cat_ref[rs, vs] += corr

        return carry

    lax.fori_loop(0, rows // chunk, chunk_body, 0)

    o = cat_ref[...]
    ms = _dot((o * o).astype(BF16), seg_ref[...])
    og = p_ref[:, C_OG:C_OG + D_MIX]
    gate = _sigmoid(og) * jnp.where(vd_ref[3:4, :] > 0.5, 1.0, og)
    cat = (o * lax.rsqrt(ms + EPS) * vd_ref[2:3, :] * gate).astype(BF16)
    hm = _dot(cat, wout_ref[...])
    y = x + _rms(hm, vd_ref[1:2, :])
    y_ref[...] = y.reshape(bb, tb, D_MODEL)


def _const_spec(shape):
    nd = len(shape)
    return pl.BlockSpec(shape, lambda b, t: (0,) * nd, pipeline_mode=pl.Buffered(1))


def _state_spec(shape, bb):
    nd = len(shape)
    return pl.BlockSpec((bb,) + tuple(shape[1:]), lambda b, t: (b,) + (0,) * (nd - 1))


_MIXER_STATES = ("s0", "s1", "s2", "ca", "cb", "nml", "mml", "cbuf")


def _mixer_call(x, lw, st, *, bb, tb):
    B, T, _ = x.shape
    chunk = min(MAX_CHUNK, tb)
    rows = bb * tb
    grid = (B // bb, T // tb)
    x_spec = pl.BlockSpec((bb, tb, D_MODEL), lambda b, t: (b, t, 0))
    consts = (lw["w_in"], lw["w_gate"], lw["w_out"], lw["seg"], lw["vec_d"], lw["vec_512"], lw["conv_p"])
    states = tuple(st[n] for n in _MIXER_STATES)
    in_specs = ([x_spec] + [_const_spec(c.shape) for c in consts]
                + [_state_spec(s.shape, bb) for s in states])
    out_shape = ([jax.ShapeDtypeStruct(x.shape, F32)]
                 + [jax.ShapeDtypeStruct(s.shape, F32) for s in states])
    out_specs = [x_spec] + [_state_spec(s.shape, bb) for s in states]
    scratch = [
        pltpu.VMEM((rows, N_PROJ), F32),
        pltpu.VMEM((rows, GH_W), F32),
        pltpu.VMEM((rows, GH_W), F32),
        pltpu.VMEM((rows, GH_W), F32),
        pltpu.VMEM((rows, MG), F32),
        pltpu.VMEM((rows, MG), F32),
        pltpu.VMEM((rows, LANES), F32),
        pltpu.VMEM((rows, LANES), F32),
        pltpu.VMEM((bb, tb + SUBLANES, 2 * MG), F32),
        pltpu.VMEM((rows, D_MIX), F32),
    ]
    outs = pl.pallas_call(
        functools.partial(_mixer_kernel, bb=bb, tb=tb, chunk=chunk),
        grid=grid, in_specs=in_specs, out_specs=out_specs, out_shape=out_shape,
        scratch_shapes=scratch,
        compiler_params=pltpu.CompilerParams(
            dimension_semantics=("arbitrary", "arbitrary"), vmem_limit_bytes=VMEM_LIMIT),
        name="mixer",
    )(x, *consts, *states)
    return outs[0], dict(zip(_MIXER_STATES, outs[1:]))


def _ffn_kernel(x_ref, wup_ref, wdn_ref, vd_ref, cp_ref, fbuf_in, y_ref, fbuf_ref, xp_ref,
                *, bb, tb):
    t = pl.program_id(1)
    rows = bb * tb
    pre = SUBLANES

    @pl.when(t == 0)
    def _():
        fbuf_ref[...] = fbuf_in[...]

    x = x_ref[...].reshape(rows, D_MODEL)
    xn = _rms(x, vd_ref[0:1, :]).astype(BF16)
    up = _dot(xn, wup_ref[...])
    xp_ref[:, 0:pre, :] = fbuf_ref[...]
    xp_ref[:, pre:pre + tb, :] = up.reshape(bb, tb, 2 * D_FF)
    conv = cp_ref[FFN_CONV:FFN_CONV + 1, :].reshape(1, 1, 2 * D_FF)
    for j in range(FFN_CONV):
        off = pre - (FFN_CONV - 1) + j
        conv = conv + xp_ref[:, off:off + tb, :] * cp_ref[j:j + 1, :].reshape(1, 1, 2 * D_FF)
    fbuf_ref[...] = xp_ref[:, tb:tb + pre, :]
    conv = conv.reshape(rows, 2 * D_FF)
    gate = conv[:, 0:D_FF]
    val = conv[:, D_FF:]
    c0 = 0.7978845608028654
    gelu = 0.5 * gate * (1.0 + jnp.tanh(c0 * (gate + 0.044715 * gate * gate * gate)))
    hmid = (gelu * val).astype(BF16)
    hd = _dot(hmid, wdn_ref[...])
    y = x + _rms(hd, vd_ref[1:2, :])
    y_ref[...] = y.reshape(bb, tb, D_MODEL)


def _ffn_call(x, lw, fbuf, *, bb, tb):
    B, T, _ = x.shape
    grid = (B // bb, T // tb)
    x_spec = pl.BlockSpec((bb, tb, D_MODEL), lambda b, t: (b, t, 0))
    consts = (lw["w_up"], lw["w_down"], lw["vec_ffn"], lw["ffn_conv_p"])
    in_specs = [x_spec] + [_const_spec(c.shape) for c in consts] + [_state_spec(fbuf.shape, bb)]
    out_shape = [jax.ShapeDtypeStruct(x.shape, F32), jax.ShapeDtypeStruct(fbuf.shape, F32)]
    out_specs = [x_spec, _state_spec(fbuf.shape, bb)]
    y, fnew = pl.pallas_call(
        functools.partial(_ffn_kernel, bb=bb, tb=tb),
        grid=grid, in_specs=in_specs, out_specs=out_specs, out_shape=out_shape,
        scratch_shapes=[pltpu.VMEM((bb, tb + SUBLANES, 2 * D_FF), F32)],
        compiler_params=pltpu.CompilerParams(
            dimension_semantics=("arbitrary", "arbitrary"), vmem_limit_bytes=VMEM_LIMIT),
        name="ffn",
    )(x, *consts, fbuf)
    return y, fnew


def _pad_rows(a, n=SUBLANES):
    return jnp.pad(a, ((0, n - a.shape[0]), (0, 0)))


def _mqk(a):
    lead = a.shape[:-1]
    parts = []
    for half in (a[..., :MQW], a[..., MQW:]):
        hh = half.reshape(lead + (ML_H, ML_DK))
        parts += [hh[..., h, :] for h in ML_ORDER] + [jnp.zeros(lead + (MG - MQW,), a.dtype)]
    return jnp.concatenate(parts, axis=-1)


def _mqk_inv(a):
    inv = np.argsort(np.asarray(ML_ORDER))
    parts = []
    for base in (0, MG):
        parts += [a[..., base + ML_DK * int(p):base + ML_DK * (int(p) + 1)] for p in inv]
    return jnp.concatenate(parts, axis=-1)


def _to_slots(a, axis):
    parts = [lax.slice_in_dim(a, _CAT_OFF[k] + SLOT * h, _CAT_OFF[k] + SLOT * (h + 1), axis=axis)
             for k, h in SLOT_HEADS]
    return jnp.concatenate(parts, axis=axis)


def _layer_weights(l, lb_all, g_mix_pre, g_mix_post, g_ffn_pre, g_ffn_post, w_in, gla_w_gate,
                   gla_b_gate, ml_conv_w, ml_conv_b, ml_b_i, ml_b_f, g_head, w_out, ffn_w_up,
                   ffn_conv_w, ffn_conv_b, ffn_w_down):
    w = w_in[l]
    w_in_p = (jnp.take(w, jnp.asarray(_PROJ_IDX), axis=1) * jnp.asarray(_PROJ_MSK)).astype(BF16)
    half = GH_W // 2
    w_gate = jnp.zeros((LANES, half), F32).at[GG_LOC:GG_LOC + GLA_RANK, 0:GQ].set(gla_w_gate[l]).astype(BF16)
    head_id = np.arange(D_MIX) // SLOT
    seg = jnp.asarray((head_id[:, None] == head_id[None, :]).astype(np.float32) / SLOT, dtype=BF16)
    vec_d = _pad_rows(jnp.stack([g_mix_pre[l], g_mix_post[l], _to_slots(g_head[l], 0),
                                 jnp.asarray(ML_LANES, F32)]))
    vec_512 = _pad_rows(jnp.stack([jnp.pad(gla_b_gate[l], (0, HQ)), jnp.pad(lb_all[l], (GQ, 0))]))
    gate_b = jnp.zeros((2 * MG,), F32)
    gate_b = gate_b.at[GATE_LOC:GATE_LOC + ML_H].set(ml_b_i[l])
    gate_b = gate_b.at[LANES + GATE_LOC:LANES + GATE_LOC + ML_H].set(ml_b_f[l])
    conv_p = _pad_rows(jnp.concatenate([_mqk(ml_conv_w[l]), _mqk(ml_conv_b[l])[None], gate_b[None]], axis=0))
    return dict(
        w_in=w_in_p, w_gate=w_gate, w_out=jnp.take(w_out[l], jnp.asarray(CAT_PERM, jnp.int32), axis=0).astype(BF16), seg=seg, vec_d=vec_d,
        vec_512=vec_512, conv_p=conv_p,
        w_up=ffn_w_up[l].astype(BF16), w_down=ffn_w_down[l].astype(BF16),
        vec_ffn=_pad_rows(jnp.stack([g_ffn_pre[l], g_ffn_post[l]])),
        ffn_conv_p=_pad_rows(jnp.concatenate([ffn_conv_w[l], ffn_conv_b[l][None]], axis=0)),
    )


def _place(block, r0, c0, shape):
    return jnp.pad(block, ((0, 0), (r0, shape[0] - r0 - block.shape[1]), (c0, shape[1] - c0 - block.shape[2])))


def _pack_states(s_gla, s_hg, c_ml, n_ml, m_ml, buf_ml, buf_ffn):
    B = s_gla.shape[0]
    st = {}
    for i, tile in enumerate(GH_TILES):
        acc = 0.0
        for qoff, dk, slot, kind, h in tile["heads"]:
            s = s_gla[:, h] if kind == "g" else s_hg[:, h]
            acc = acc + _place(jnp.swapaxes(s, 1, 2), slot * SLOT, qoff, (TILE, tile["qw"]))
        st["s%d" % i] = acc
    for name, grp in zip(("ca", "cb"), ML_GROUPS):
        acc = 0.0
        for qoff, slot, h in grp["heads"]:
            acc = acc + _place(c_ml[:, h], qoff, slot * SLOT, (grp["qw"], TILE))
        st[name] = acc
    n_perm = jnp.concatenate([n_ml[:, h] for h in ML_ORDER], axis=-1)
    st["nml"] = jnp.pad(n_perm, ((0, 0), (0, MG - MQW)))[:, None, :]
    st["mml"] = jnp.pad(m_ml, ((0, 0), (0, LANES - ML_H)))[:, None, :]
    st["cbuf"] = jnp.pad(_mqk(buf_ml), ((0, 0), (SUBLANES - (ML_CONV - 1), 0), (0, 0)))
    fbuf = jnp.pad(buf_ffn, ((0, 0), (SUBLANES - (FFN_CONV - 1), 0), (0, 0)))
    return st, fbuf


def _zero_states(B):
    st = {"s%d" % i: jnp.zeros((B, TILE, tile["qw"]), F32) for i, tile in enumerate(GH_TILES)}
    for name, grp in zip(("ca", "cb"), ML_GROUPS):
        st[name] = jnp.zeros((B, grp["qw"], TILE), F32)
    st["nml"] = jnp.zeros((B, 1, MG), F32)
    st["mml"] = jnp.zeros((B, 1, LANES), F32)
    st["cbuf"] = jnp.zeros((B, SUBLANES, 2 * MG), F32)
    return st, jnp.zeros((B, SUBLANES, 2 * D_FF), F32)


def _unpack_states(st, fbuf):
    gl, hg, cm = {}, {}, {}
    for i, tile in enumerate(GH_TILES):
        s_t = st["s%d" % i]
        for qoff, dk, slot, kind, h in tile["heads"]:
            s = jnp.swapaxes(s_t[:, slot * SLOT:(slot + 1) * SLOT, qoff:qoff + dk], 1, 2)
            (gl if kind == "g" else hg)[h] = s
    for name, grp in zip(("ca", "cb"), ML_GROUPS):
        for qoff, slot, h in grp["heads"]:
            cm[h] = st[name][:, qoff:qoff + ML_DK, slot * SLOT:(slot + 1) * SLOT]
    n = st["nml"][:, 0, :]
    pos = {h: i for i, h in enumerate(ML_ORDER)}
    n_ml = jnp.stack([n[:, ML_DK * pos[h]:ML_DK * (pos[h] + 1)] for h in range(ML_H)], axis=1)
    m_ml = st["mml"][:, 0, :ML_H]
    buf_ml = _mqk_inv(st["cbuf"][:, SUBLANES - (ML_CONV - 1):, :])
    buf_ffn = fbuf[:, SUBLANES - (FFN_CONV - 1):, :]
    return (jnp.stack([gl[h] for h in range(GLA_H)], 1), jnp.stack([hg[h] for h in range(HG_H)], 1),
            jnp.stack([cm[h] for h in range(ML_H)], 1), n_ml, m_ml, buf_ml, buf_ffn)


def _block_sizes(B, T):
    if T % 512 == 0:
        return 1, 512, 512
    if T % 256 == 0:
        return 1, 256, 256
    bb = 4 if B % 4 == 0 else 1
    return bb, T, T


def _run_trunk(x, states, weights):
    B, T, _ = x.shape
    bb, tb_mix, tb_ffn = _block_sizes(B, T)
    new = [[] for _ in range(7)]
    for l in range(DEPTH):
        if states is None:
            st, fbuf = _zero_states(B)
        else:
            st, fbuf = _pack_states(*(s[l] for s in states))
        x, st = _mixer_call(x, weights[l], st, bb=bb, tb=tb_mix)
        x, fbuf = _ffn_call(x, weights[l], fbuf, bb=bb, tb=tb_ffn)
        for i, s in enumerate(_unpack_states(st, fbuf)):
            new[i].append(s)
    return x, [jnp.stack(s, axis=0) for s in new]


def kernel(x_prompt, x_sample, state_gla, state_hgrn, state_mlstm_C, state_mlstm_n,
           state_mlstm_m, cache_mlstm_conv, cache_ffn_conv, g_mix_pre, g_mix_post, g_ffn_pre,
           g_ffn_post, w_in, gla_w_gate, gla_b_gate, hgrn_lb, ml_conv_w, ml_conv_b, ml_b_i,
           ml_b_f, g_head, w_out, ffn_w_up, ffn_conv_w, ffn_conv_b, ffn_w_down):
    sm = jax.nn.softmax(hgrn_lb.astype(F32), axis=0)
    lb_all = jnp.cumsum(sm, axis=0) - sm[0:1]
    weights = [
        _layer_weights(l, lb_all, g_mix_pre, g_mix_post, g_ffn_pre, g_ffn_post, w_in, gla_w_gate,
                       gla_b_gate, ml_conv_w, ml_conv_b, ml_b_i, ml_b_f, g_head, w_out, ffn_w_up,
                       ffn_conv_w, ffn_conv_b, ffn_w_down)
        for l in range(DEPTH)]
    y_prompt, ps = _run_trunk(x_prompt, None, weights)
    sample_states = (state_gla, state_hgrn, state_mlstm_C, state_mlstm_n, state_mlstm_m,
                     cache_mlstm_conv, cache_ffn_conv)
    y_sample, ss = _run_trunk(x_sample, sample_states, weights)
    return (y_prompt, y_sample, *ps, *ss)
```

```python
import functools

import numpy as np
import jax
import jax.numpy as jnp
from jax import lax
from jax.experimental import pallas as pl
from jax.experimental.pallas import tpu as pltpu

F32 = jnp.float32
BF16 = jnp.bfloat16
HIGHEST = lax.Precision.HIGHEST

D_MODEL = 1024
DEPTH = 2
GLA_H, GLA_DK, GLA_DV, GLA_RANK, GLA_GATE_NORM = 6, 32, 64, 16, 16.0
HG_H, HG_DK, HG_DV = 5, 64, 64
ML_H, ML_DK, ML_DV, ML_CONV = 5, 64, 64, 4
D_MIX = GLA_H * GLA_DV + HG_H * HG_DV + ML_H * ML_DV
D_FF = 2816
FFN_CONV = 3
EPS = 1e-6
NEG_BIG = -1e30

LANES = 128
SUBLANES = 8
TILE = 256
SLOT = 64
SUB = 16
MAX_CHUNK = 64
EXP_CLAMP = 80.0
VMEM_LIMIT = 56 * 1024 * 1024

GQ, HQ = GLA_H * GLA_DK, HG_H * HG_DK
GV, HV, MV = GLA_H * GLA_DV, HG_H * HG_DV, ML_H * ML_DV
MQW = ML_H * ML_DK
GH_W = GQ + HQ
MG = 384
C_Q = 0
C_K = C_Q + GH_W
C_V = C_K + GH_W
C_OG = C_V + D_MIX
C_MQ = C_OG + D_MIX
C_MK = C_MQ + MG
N_PROJ = C_MK + MG
SMALL_OFF = 256
GG_LOC = MQW - SMALL_OFF
GATE_LOC = GG_LOC + GLA_RANK
ML_ORDER = (1, 2, 3, 4, 0)

_SPLITS = (GQ, GQ, GV, GLA_RANK, GV, HQ, HQ, HV, HV, MQW, MQW, MV, ML_H, ML_H, MV)
_NAMES = ("gq", "gk", "gv", "gg", "gr", "hq", "hf", "hi", "hg", "mq", "mk", "mv", "mi", "mf", "mo")
_SRC = {}
_acc = 0
for _n, _s in zip(_NAMES, _SPLITS):
    _SRC[_n] = _acc
    _acc += _s
N_IN = _acc

SLOT_HEADS = (("g", 0), ("g", 1), ("g", 2), ("g", 3),
              ("g", 4), ("g", 5), ("h", 0), ("m", 0),
              ("h", 1), ("h", 2), ("h", 3), ("h", 4),
              ("m", 1), ("m", 2), ("m", 3), ("m", 4))
_CAT_OFF = {"g": 0, "h": GV, "m": GV + HV}
CAT_PERM = np.concatenate([_CAT_OFF[k] + SLOT * h + np.arange(SLOT) for k, h in SLOT_HEADS])
ML_LANES = np.concatenate([np.full(SLOT, k == "m") for k, _ in SLOT_HEADS])

GH_TILES = (
    dict(q0=0, qw=128, vt=0, heads=tuple((32 * r, GLA_DK, r, "g", r) for r in range(4))),
    dict(q0=128, qw=128, vt=1, heads=((0, GLA_DK, 0, "g", 4), (32, GLA_DK, 1, "g", 5), (64, HG_DK, 2, "h", 0))),
    dict(q0=256, qw=256, vt=2, heads=tuple((64 * r, HG_DK, r, "h", r + 1) for r in range(4))),
)
ML_GROUPS = (
    dict(q0=0, qw=256, vt=3, heads=tuple((64 * r, r, r + 1) for r in range(4))),
    dict(q0=256, qw=128, vt=1, heads=((0, 3, 0),)),
)


def _proj_segments():
    segs = [(_SRC["gq"], GQ), (_SRC["hq"], HQ), (_SRC["gk"], GQ), (_SRC["hf"], HQ)]
    for names in (("gv", "hi", "mv"), ("gr", "hg", "mo")):
        src = dict(zip("ghm", names))
        for kind, h in SLOT_HEADS:
            segs.append((_SRC[src[kind]] + SLOT * h, SLOT))
    pad = MG - MQW - GLA_RANK - ML_H
    for name, gate in (("mq", "mi"), ("mk", "mf")):
        for h in ML_ORDER:
            segs.append((_SRC[name] + ML_DK * h, ML_DK))
        segs.append((_SRC["gg"], GLA_RANK) if name == "mq" else (None, GLA_RANK))
        segs.append((_SRC[gate], ML_H))
        segs.append((None, pad))
    merged = []
    for s, n in segs:
        if merged and s is not None and merged[-1][0] is not None and merged[-1][0] + merged[-1][1] == s:
            merged[-1] = (merged[-1][0], merged[-1][1] + n)
        else:
            merged.append((s, n))
    assert sum(n for _, n in merged) == N_PROJ
    return merged


_PROJ_SEGS = _proj_segments()
_PROJ_IDX = np.concatenate([np.arange(s, s + n) if s is not None else np.zeros(n, np.int64)
                            for s, n in _PROJ_SEGS]).astype(np.int32)
_PROJ_MSK = np.concatenate([np.full(n, 0.0 if s is None else 1.0, np.float32) for s, n in _PROJ_SEGS])


def _sigmoid(x):
    return 1.0 / (1.0 + jnp.exp(-x))


def _log_sigmoid(x):
    return jnp.minimum(x, 0.0) - jnp.log(1.0 + jnp.exp(-jnp.abs(x)))


def _rms(x, g):
    return x * lax.rsqrt(jnp.mean(x * x, axis=-1, keepdims=True) + EPS) * g


def _split3(x):
    hi = x.astype(BF16)
    r1 = x - hi.astype(F32)
    mid = r1.astype(BF16)
    lo = (r1 - mid.astype(F32)).astype(BF16)
    return hi, mid, lo


def _in_range(idx, lo, width):
    return (idx >= lo) & (idx < lo + width)


def _dot(a, b, precision=None):
    return jnp.dot(a, b, preferred_element_type=F32, precision=precision)


def _dot_nt(a, b):
    return lax.dot_general(a, b, (((1,), (1,)), ((), ())), preferred_element_type=F32)


def _dot_tn(a, b, precision=None):
    return lax.dot_general(a, b, (((0,), (0,)), ((), ())), preferred_element_type=F32,
                           precision=precision)


def _mixer_kernel(x_ref, win_ref, wg_ref, wout_ref, seg_ref, vd_ref, v512_ref, cp_ref,
                  s0_in, s1_in, s2_in, ca_in, cb_in, nml_in, mml_in, cbuf_in,
                  y_ref, s0_ref, s1_ref, s2_ref, ca_ref, cb_ref, nml_ref, mml_ref, cbuf_ref,
                  p_ref, q_ref, k_ref, la_ref, mq_ref, mk_ref, ig_ref, lf_ref, xp_ref, cat_ref,
                  *, bb, tb, chunk):
    t = pl.program_id(1)
    rows = bb * tb
    nb = chunk // SUB
    state_pairs = ((s0_in, s0_ref), (s1_in, s1_ref), (s2_in, s2_ref), (ca_in, ca_ref),
                   (cb_in, cb_ref), (nml_in, nml_ref), (mml_in, mml_ref), (cbuf_in, cbuf_ref))

    @pl.when(t == 0)
    def _():
        for src, dst in state_pairs:
            dst[...] = src[...]

    x = x_ref[...].reshape(rows, D_MODEL)
    xn = _rms(x, vd_ref[0:1, :]).astype(BF16)
    p_ref[...] = _dot(xn, win_ref[...])

    half = GH_W // 2
    small_q = p_ref[:, C_MQ + SMALL_OFF:C_MQ + MG]
    z = _dot(small_q.astype(BF16), wg_ref[...]) + v512_ref[0:1, 0:half]
    la_gla = _log_sigmoid(z) * (1.0 / GLA_GATE_NORM)
    is_gla = lax.broadcasted_iota(jnp.int32, (1, half), 1) < GQ
    for c0 in (0, half):
        qraw = p_ref[:, C_Q + c0:C_Q + c0 + half]
        fr = p_ref[:, C_K + c0:C_K + c0 + half]
        lb = v512_ref[1:2, c0:c0 + half]
        q_h = qraw * _sigmoid(qraw) * (HG_DK ** -0.5)
        la_h = jnp.log(lb + (1.0 - lb) * _sigmoid(fr))
        k_h = (1.0 - lb) * _sigmoid(-fr)
        if c0 == 0:
            q_ref[:, 0:half] = jnp.where(is_gla, qraw * (GLA_DK ** -0.5), q_h)
            k_ref[:, 0:half] = jnp.where(is_gla, fr, k_h)
            la_ref[:, 0:half] = jnp.where(is_gla, la_gla, la_h)
        else:
            q_ref[:, half:] = q_h
            k_ref[:, half:] = k_h
            la_ref[:, half:] = la_h

    pre = SUBLANES
    xp_ref[:, 0:pre, :] = cbuf_ref[...]
    xp_ref[:, pre:pre + tb, :] = p_ref[:, C_MQ:C_MQ + 2 * MG].reshape(bb, tb, 2 * MG)
    conv = cp_ref[ML_CONV:ML_CONV + 1, :].reshape(1, 1, 2 * MG)
    for j in range(ML_CONV):
        off = pre - (ML_CONV - 1) + j
        conv = conv + xp_ref[:, off:off + tb, :] * cp_ref[j:j + 1, :].reshape(1, 1, 2 * MG)
    cbuf_ref[...] = xp_ref[:, tb:tb + pre, :]
    qk = (conv * _sigmoid(conv)).reshape(rows, 2 * MG)
    mq_ref[...] = qk[:, 0:MG]
    mk_ref[...] = qk[:, MG:] * (ML_DK ** -0.5)
    ig_ref[...] = small_q + cp_ref[ML_CONV + 1:ML_CONV + 2, 0:LANES]
    lf_ref[...] = _log_sigmoid(p_ref[:, C_MK + SMALL_OFF:C_MK + MG] + cp_ref[ML_CONV + 1:ML_CONV + 2, LANES:2 * LANES])

    sub_sh = SUB.bit_length() - 1
    chunk_sh = chunk.bit_length() - 1
    assert (1 << sub_sh) == SUB and (1 << chunk_sh) == chunk
    ri = lax.broadcasted_iota(jnp.int32, (chunk, chunk), 0)
    ci = lax.broadcasted_iota(jnp.int32, (chunk, chunk), 1)
    causal = ci <= ri
    tri = causal.astype(BF16)
    tri_blk = (causal & ((ri >> sub_sh) == (ci >> sub_sh))).astype(BF16)
    tri3 = jnp.concatenate([tri] * 3, axis=1)
    cum3 = jnp.concatenate([tri3, jnp.concatenate([tri_blk] * 3, axis=1)], axis=0)
    li = lax.broadcasted_iota(jnp.int32, (LANES, 3 * LANES), 0)
    lj = lax.broadcasted_iota(jnp.int32, (LANES, 3 * LANES), 1) & (LANES - 1)
    eye3 = (li == lj).astype(BF16)
    max_heads = TILE // SLOT
    sr = lax.broadcasted_iota(jnp.int32, (max_heads * chunk, nb * chunk), 0) & (chunk - 1)
    sc = lax.broadcasted_iota(jnp.int32, (max_heads * chunk, nb * chunk), 1)
    slab, col = sc >> chunk_sh, sc & (chunk - 1)
    off_ok = (slab < nb - 1) & ((sr >> sub_sh) == slab + 1) & (col < (slab + 1) * SUB)
    diag_ok = (slab == nb - 1) & ((sr >> sub_sh) == (col >> sub_sh)) & (col <= sr)
    row_id = lax.broadcasted_iota(jnp.int32, (chunk, 1), 0)
    sub_pos = row_id & (SUB - 1)
    sub_last = sub_pos == SUB - 1
    lane_v = lax.broadcasted_iota(jnp.int32, (1, TILE), 1)
    lane_m = lax.broadcasted_iota(jnp.int32, (1, LANES), 1)
    gh_refs = (s0_ref, s1_ref, s2_ref)
    ml_refs = (ca_ref, cb_ref)
    chunks_per_stream = tb // chunk

    def chunk_body(ic, carry):
        b_i = ic // chunks_per_stream
        rs = pl.ds(pl.multiple_of(ic * chunk, chunk), chunk)
        v_all = p_ref[rs, C_V:C_V + D_MIX].astype(BF16)
        out_tiles = [None] * (D_MIX // TILE)

        def add_out(vt, val):
            out_tiles[vt] = val if out_tiles[vt] is None else out_tiles[vt] + val

        la = la_ref[rs, :]
        cums = _dot(cum3, jnp.concatenate(_split3(la), axis=0))
        b = cums[0:chunk]
        bq = cums[chunk:]
        q = q_ref[rs, :]
        k = k_ref[rs, :]
        b_last = b[chunk - 1:chunk, :]
        q_in = (q * jnp.exp(bq)).astype(BF16)
        q_st = (q * jnp.exp(b)).astype(BF16)
        k_st = (k * jnp.exp(b_last - b)).astype(BF16)
        dec = jnp.exp(b_last)
        pieces = []
        for i in range(1, nb):
            r_i = b[i * SUB - 1:i * SUB, :]
            kt = k * jnp.exp(jnp.minimum(r_i - b, 0.0))
            pieces.append(jnp.where(row_id < i * SUB, kt, 0.0))
        pieces.append(k * jnp.exp(jnp.minimum(-bq, EXP_CLAMP)))
        worst = jnp.max(jnp.where(sub_last, -bq, 0.0))
        score_sel = off_ok | (diag_ok & (worst <= EXP_CLAMP))
        k_stack = jnp.concatenate(pieces, axis=0).astype(BF16)
        v_rep = jnp.concatenate([v_all] * nb, axis=0)

        for tile, s_ref in zip(GH_TILES, gh_refs):
            q0, qw, vt, heads = tile["q0"], tile["qw"], tile["vt"], tile["heads"]
            nr = len(heads)
            qs = slice(q0, q0 + qw)
            vs = slice(vt * TILE, (vt + 1) * TILE)
            lane_q = lax.broadcasted_iota(jnp.int32, (1, qw), 1)
            row_s = lax.broadcasted_iota(jnp.int32, (TILE, 1), 0)
            qmasks = [_in_range(lane_q, qoff, dk) for qoff, dk, _, _, _ in heads]
            lhs_in = jnp.concatenate([jnp.where(m, q_in[:, qs], 0) for m in qmasks], axis=0)
            lhs_st = jnp.concatenate([jnp.where(m, q_st[:, qs], 0) for m in qmasks], axis=0)
            scores = _dot_nt(lhs_in, k_stack[:, qs])
            scores = jnp.where(score_sel[0:nr * chunk], scores, 0.0).astype(BF16)
            s_t = s_ref[b_i]
            pv = _dot(scores, v_rep[:, vs]) + _dot_nt(lhs_st, s_t.astype(BF16))
            block = None
            for r, (qoff, dk, slot, _, _) in enumerate(heads):
                add_out(vt, jnp.where(_in_range(lane_v, slot * SLOT, SLOT), pv[r * chunk:(r + 1) * chunk], 0.0))
                m = _in_range(row_s, slot * SLOT, SLOT) & qmasks[r]
                block = m if block is None else block | m
            upd = _dot_tn(v_all[:, vs], k_st[:, qs])
            s_ref[b_i] = s_t * dec[:, qs] + jnp.where(block, upd, 0.0)

        lf = lf_ref[rs, :]
        ig = ig_ref[rs, :]
        bm = _dot(tri3, jnp.concatenate(_split3(lf), axis=0))
        u_t = _dot_nt(eye3, jnp.concatenate(_split3(ig - bm), axis=1))
        m_all = mml_ref[b_i]
        m_new = m_all
        for grp, c_ref in zip(ML_GROUPS, ml_refs):
            q0, qw, vt, heads = grp["q0"], grp["qw"], grp["vt"], grp["heads"]
            qs = slice(q0, q0 + qw)
            vs = slice(vt * TILE, (vt + 1) * TILE)
            lane_q = lax.broadcasted_iota(jnp.int32, (1, qw), 1)
            row_q = lax.broadcasted_iota(jnp.int32, (qw, 1), 0)
            mq = mq_ref[rs, qs]
            mk = mk_ref[rs, qs]
            n_row = nml_ref[b_i, :, qs]
            lhs, w_st, w0_st, mi_st = [], [], [], []
            wk_l = jnp.zeros((chunk, qw), F32)
            dc_q = jnp.zeros((1, qw), F32)
            dc_v = jnp.zeros((1, TILE), F32)
            block = None
            for qoff, slot, h in heads:
                qm = _in_range(lane_q, qoff, ML_DK)
                vm = _in_range(lane_v, slot * SLOT, SLOT)
                gl = GATE_LOC + h
                bcol = bm[:, gl:gl + 1]
                m_prev = m_all[:, h:h + 1]
                dlog = jnp.where(causal, bcol + u_t[gl:gl + 1, :], NEG_BIG)
                g = bcol + m_prev
                m_i = jnp.maximum(g, jnp.max(dlog, axis=1, keepdims=True))
                w_st.append(jnp.where(causal, jnp.exp(dlog - m_i), 0.0))
                w0_st.append(jnp.exp(g - m_i))
                mi_st.append(m_i)
                lhs.append(jnp.where(qm, mq, 0.0))
                m_last = m_i[chunk - 1:chunk, :]
                bl = bcol[chunk - 1:chunk, :]
                wk = jnp.exp(bl - bcol + ig[:, gl:gl + 1] - m_last)
                dc = jnp.exp(bl + m_prev - m_last)
                wk_l = jnp.where(qm, wk, wk_l)
                dc_q = jnp.where(qm, dc, dc_q)
                dc_v = jnp.where(vm, dc, dc_v)
                m_new = jnp.where(lane_m == h, m_last, m_new)
                bm_ = _in_range(row_q, qoff, ML_DK) & vm
                block = bm_ if block is None else block | bm_
            lhs = jnp.concatenate(lhs, axis=0)
            w_st = jnp.concatenate(w_st, axis=0)
            w0_st = jnp.concatenate(w0_st, axis=0)
            mi_st = jnp.concatenate(mi_st, axis=0)
            lhs_b = lhs.astype(BF16)
            s = _dot_nt(lhs_b, mk.astype(BF16)) * w_st
            c_st = c_ref[b_i]
            num = _dot(s.astype(BF16), v_all[:, vs]) + w0_st * _dot(lhs_b, c_st.astype(BF16))
            den = (jnp.sum(s, axis=1, keepdims=True)
                   + w0_st * jnp.sum(lhs * n_row, axis=1, keepdims=True))
            hout = num / jnp.maximum(jnp.abs(den), jnp.exp(-mi_st))
            for r, (qoff, slot, h) in enumerate(heads):
                add_out(vt, jnp.where(_in_range(lane_v, slot * SLOT, SLOT), hout[r * chunk:(r + 1) * chunk], 0.0))
            kw = mk * wk_l
            c_ref[b_i] = c_st * dc_v + jnp.where(block, _dot_tn(kw.astype(BF16), v_all[:, vs]), 0.0)
            nml_ref[b_i, :, qs] = n_row * dc_q + jnp.sum(kw, axis=0, keepdims=True)
        mml_ref[b_i] = m_new

        for vt, val in enumerate(out_tiles):
            cat_ref[rs, vt * TILE:(vt + 1) * TILE] = val

        @pl.when(worst > EXP_CLAMP)
        def _():
            hl = lax.broadcasted_iota(jnp.int32, (3 * GH_W, LANES), 0) & (GH_W - 1)
            hc = lax.broadcasted_iota(jnp.int32, (3 * GH_W, LANES), 1)
            head_of_lane = jnp.where(hl < GQ, hl >> (GLA_DK.bit_length() - 1),
                                     GLA_H + ((hl - GQ) >> (HG_DK.bit_length() - 1)))
            seg_q3 = (head_of_lane == hc).astype(BF16)
            bands = []
            for s in range(SUB):
                k_s = k if s == 0 else pltpu.roll(k, s, 0)
                b_s = b if s == 0 else pltpu.roll(b, s, 0)
                pair = q * k_s * jnp.exp(jnp.minimum(b - b_s, 0.0))
                pair = jnp.where(sub_pos >= s, pair, 0.0)
                bands.append(_dot(jnp.concatenate(_split3(pair), axis=1), seg_q3))
            for tile in GH_TILES:
                vs = slice(tile["vt"] * TILE, (tile["vt"] + 1) * TILE)
                corr = None
                for qoff, dk, slot, kind, h in tile["heads"]:
                    hcol = h if kind == "g" else GLA_H + h
                    d = jnp.zeros((chunk, chunk), F32)
                    for s in range(SUB):
                        d = jnp.where(ci == ri - s, bands[s][:, hcol:hcol + 1], d)
                    o = _dot(d.astype(BF16), v_all[:, vs])
                    o = jnp.where(_in_range(lane_v, slot * SLOT, SLOT), o, 0.0)
                    corr = o if corr is None else corr + o
                cat_ref[rs, vs] += corr

        return carry

    lax.fori_loop(0, rows // chunk, chunk_body, 0)

    o = cat_ref[...]
    oo = (o * o).astype(BF16)
    ms = jnp.concatenate([_dot(oo[:, i * TILE:(i + 1) * TILE], seg_ref[...])
                          for i in range(D_MIX // TILE)], axis=1)
    og = p_ref[:, C_OG:C_OG + D_MIX]
    gate = _sigmoid(og) * jnp.where(vd_ref[3:4, :] > 0.5, 1.0, og)
    cat = (o * lax.rsqrt(ms + EPS) * vd_ref[2:3, :] * gate).astype(BF16)
    hm = _dot(cat, wout_ref[...])
    y = x + _rms(hm, vd_ref[1:2, :])
    y_ref[...] = y.reshape(bb, tb, D_MODEL)


def _const_spec(shape):
    nd = len(shape)
    return pl.BlockSpec(shape, lambda b, t: (0,) * nd, pipeline_mode=pl.Buffered(1))


def _state_spec(shape, bb):
    nd = len(shape)
    return pl.BlockSpec((bb,) + tuple(shape[1:]), lambda b, t: (b,) + (0,) * (nd - 1))


_MIXER_STATES = ("s0", "s1", "s2", "ca", "cb", "nml", "mml", "cbuf")


def _mixer_call(x, lw, st, *, bb, tb):
    B, T, _ = x.shape
    chunk = min(MAX_CHUNK, tb)
    rows = bb * tb
    grid = (B // bb, T // tb)
    x_spec = pl.BlockSpec((bb, tb, D_MODEL), lambda b, t: (b, t, 0))
    consts = (lw["w_in"], lw["w_gate"], lw["w_out"], lw["seg"], lw["vec_d"], lw["vec_512"], lw["conv_p"])
    states = tuple(st[n] for n in _MIXER_STATES)
    in_specs = ([x_spec] + [_const_spec(c.shape) for c in consts]
                + [_state_spec(s.shape, bb) for s in states])
    out_shape = ([jax.ShapeDtypeStruct(x.shape, F32)]
                 + [jax.ShapeDtypeStruct(s.shape, F32) for s in states])
    out_specs = [x_spec] + [_state_spec(s.shape, bb) for s in states]
    scratch = [
        pltpu.VMEM((rows, N_PROJ), F32),
        pltpu.VMEM((rows, GH_W), F32),
        pltpu.VMEM((rows, GH_W), F32),
        pltpu.VMEM((rows, GH_W), F32),
        pltpu.VMEM((rows, MG), F32),
        pltpu.VMEM((rows, MG), F32),
        pltpu.VMEM((rows, LANES), F32),
        pltpu.VMEM((rows, LANES), F32),
        pltpu.VMEM((bb, tb + SUBLANES, 2 * MG), F32),
        pltpu.VMEM((rows, D_MIX), F32),
    ]
    outs = pl.pallas_call(
        functools.partial(_mixer_kernel, bb=bb, tb=tb, chunk=chunk),
        grid=grid, in_specs=in_specs, out_specs=out_specs, out_shape=out_shape,
        scratch_shapes=scratch,
        compiler_params=pltpu.CompilerParams(
            dimension_semantics=("arbitrary", "arbitrary"), vmem_limit_bytes=VMEM_LIMIT),
        name="mixer",
    )(x, *consts, *states)
    return outs[0], dict(zip(_MIXER_STATES, outs[1:]))


def _ffn_kernel(x_ref, wup_ref, wdn_ref, vd_ref, cp_ref, fbuf_in, y_ref, fbuf_ref, xp_ref,
                *, bb, tb):
    t = pl.program_id(1)
    rows = bb * tb
    pre = SUBLANES

    @pl.when(t == 0)
    def _():
        fbuf_ref[...] = fbuf_in[...]

    x = x_ref[...].reshape(rows, D_MODEL)
    xn = _rms(x, vd_ref[0:1, :]).astype(BF16)
    up = _dot(xn, wup_ref[...])
    xp_ref[:, 0:pre, :] = fbuf_ref[...]
    xp_ref[:, pre:pre + tb, :] = up.reshape(bb, tb, 2 * D_FF)
    conv = cp_ref[FFN_CONV:FFN_CONV + 1, :].reshape(1, 1, 2 * D_FF)
    for j in range(FFN_CONV):
        off = pre - (FFN_CONV - 1) + j
        conv = conv + xp_ref[:, off:off + tb, :] * cp_ref[j:j + 1, :].reshape(1, 1, 2 * D_FF)
    fbuf_ref[...] = xp_ref[:, tb:tb + pre, :]
    conv = conv.reshape(rows, 2 * D_FF)
    gate = conv[:, 0:D_FF]
    val = conv[:, D_FF:]
    c0 = 0.7978845608028654
    gelu = 0.5 * gate * (1.0 + jnp.tanh(c0 * (gate + 0.044715 * gate * gate * gate)))
    hmid = (gelu * val).astype(BF16)
    hd = _dot(hmid, wdn_ref[...])
    y = x + _rms(hd, vd_ref[1:2, :])
    y_ref[...] = y.reshape(bb, tb, D_MODEL)


def _ffn_call(x, lw, fbuf, *, bb, tb):
    B, T, _ = x.shape
    grid = (B // bb, T // tb)
    x_spec = pl.BlockSpec((bb, tb, D_MODEL), lambda b, t: (b, t, 0))
    consts = (lw["w_up"], lw["w_down"], lw["vec_ffn"], lw["ffn_conv_p"])
    in_specs = [x_spec] + [_const_spec(c.shape) for c in consts] + [_state_spec(fbuf.shape, bb)]
    out_shape = [jax.ShapeDtypeStruct(x.shape, F32), jax.ShapeDtypeStruct(fbuf.shape, F32)]
    out_specs = [x_spec, _state_spec(fbuf.shape, bb)]
    y, fnew = pl.pallas_call(
        functools.partial(_ffn_kernel, bb=bb, tb=tb),
        grid=grid, in_specs=in_specs, out_specs=out_specs, out_shape=out_shape,
        scratch_shapes=[pltpu.VMEM((bb, tb + SUBLANES, 2 * D_FF), F32)],
        compiler_params=pltpu.CompilerParams(
            dimension_semantics=("arbitrary", "arbitrary"), vmem_limit_bytes=VMEM_LIMIT),
        name="ffn",
    )(x, *consts, fbuf)
    return y, fnew


def _pad_rows(a, n=SUBLANES):
    return jnp.pad(a, ((0, n - a.shape[0]), (0, 0)))


def _mqk(a):
    lead = a.shape[:-1]
    parts = []
    for half in (a[..., :MQW], a[..., MQW:]):
        hh = half.reshape(lead + (ML_H, ML_DK))
        parts += [hh[..., h, :] for h in ML_ORDER] + [jnp.zeros(lead + (MG - MQW,), a.dtype)]
    return jnp.concatenate(parts, axis=-1)


def _mqk_inv(a):
    inv = np.argsort(np.asarray(ML_ORDER))
    parts = []
    for base in (0, MG):
        parts += [a[..., base + ML_DK * int(p):base + ML_DK * (int(p) + 1)] for p in inv]
    return jnp.concatenate(parts, axis=-1)


def _to_slots(a, axis):
    parts = [lax.slice_in_dim(a, _CAT_OFF[k] + SLOT * h, _CAT_OFF[k] + SLOT * (h + 1), axis=axis)
             for k, h in SLOT_HEADS]
    return jnp.concatenate(parts, axis=axis)


def _layer_weights(l, lb_all, g_mix_pre, g_mix_post, g_ffn_pre, g_ffn_post, w_in, gla_w_gate,
                   gla_b_gate, ml_conv_w, ml_conv_b, ml_b_i, ml_b_f, g_head, w_out, ffn_w_up,
                   ffn_conv_w, ffn_conv_b, ffn_w_down):
    w = w_in[l]
    w_in_p = (jnp.take(w, jnp.asarray(_PROJ_IDX), axis=1) * jnp.asarray(_PROJ_MSK)).astype(BF16)
    half = GH_W // 2
    w_gate = jnp.zeros((LANES, half), F32).at[GG_LOC:GG_LOC + GLA_RANK, 0:GQ].set(gla_w_gate[l]).astype(BF16)
    head_id = np.arange(TILE) // SLOT
    seg = jnp.asarray((head_id[:, None] == head_id[None, :]).astype(np.float32) / SLOT, dtype=BF16)
    vec_d = _pad_rows(jnp.stack([g_mix_pre[l], g_mix_post[l], _to_slots(g_head[l], 0),
                                 jnp.asarray(ML_LANES, F32)]))
    vec_512 = _pad_rows(jnp.stack([jnp.pad(gla_b_gate[l], (0, HQ)), jnp.pad(lb_all[l], (GQ, 0))]))
    gate_b = jnp.zeros((2 * MG,), F32)
    gate_b = gate_b.at[GATE_LOC:GATE_LOC + ML_H].set(ml_b_i[l])
    gate_b = gate_b.at[LANES + GATE_LOC:LANES + GATE_LOC + ML_H].set(ml_b_f[l])
    conv_p = _pad_rows(jnp.concatenate([_mqk(ml_conv_w[l]), _mqk(ml_conv_b[l])[None], gate_b[None]], axis=0))
    return dict(
        w_in=w_in_p, w_gate=w_gate, w_out=jnp.take(w_out[l], jnp.asarray(CAT_PERM, jnp.int32), axis=0).astype(BF16), seg=seg, vec_d=vec_d,
        vec_512=vec_512, conv_p=conv_p,
        w_up=ffn_w_up[l].astype(BF16), w_down=ffn_w_down[l].astype(BF16),
        vec_ffn=_pad_rows(jnp.stack([g_ffn_pre[l], g_ffn_post[l]])),
        ffn_conv_p=_pad_rows(jnp.concatenate([ffn_conv_w[l], ffn_conv_b[l][None]], axis=0)),
    )


def _place(block, r0, c0, shape):
    return jnp.pad(block, ((0, 0), (r0, shape[0] - r0 - block.shape[1]), (c0, shape[1] - c0 - block.shape[2])))


def _pack_states(s_gla, s_hg, c_ml, n_ml, m_ml, buf_ml, buf_ffn):
    B = s_gla.shape[0]
    st = {}
    for i, tile in enumerate(GH_TILES):
        acc = 0.0
        for qoff, dk, slot, kind, h in tile["heads"]:
            s = s_gla[:, h] if kind == "g" else s_hg[:, h]
            acc = acc + _place(jnp.swapaxes(s, 1, 2), slot * SLOT, qoff, (TILE, tile["qw"]))
        st["s%d" % i] = acc
    for name, grp in zip(("ca", "cb"), ML_GROUPS):
        acc = 0.0
        for qoff, slot, h in grp["heads"]:
            acc = acc + _place(c_ml[:, h], qoff, slot * SLOT, (grp["qw"], TILE))
        st[name] = acc
    n_perm = jnp.concatenate([n_ml[:, h] for h in ML_ORDER], axis=-1)
    st["nml"] = jnp.pad(n_perm, ((0, 0), (0, MG - MQW)))[:, None, :]
    st["mml"] = jnp.pad(m_ml, ((0, 0), (0, LANES - ML_H)))[:, None, :]
    st["cbuf"] = jnp.pad(_mqk(buf_ml), ((0, 0), (SUBLANES - (ML_CONV - 1), 0), (0, 0)))
    fbuf = jnp.pad(buf_ffn, ((0, 0), (SUBLANES - (FFN_CONV - 1), 0), (0, 0)))
    return st, fbuf


def _zero_states(B):
    st = {"s%d" % i: jnp.zeros((B, TILE, tile["qw"]), F32) for i, tile in enumerate(GH_TILES)}
    for name, grp in zip(("ca", "cb"), ML_GROUPS):
        st[name] = jnp.zeros((B, grp["qw"], TILE), F32)
    st["nml"] = jnp.zeros((B, 1, MG), F32)
    st["mml"] = jnp.zeros((B, 1, LANES), F32)
    st["cbuf"] = jnp.zeros((B, SUBLANES, 2 * MG), F32)
    return st, jnp.zeros((B, SUBLANES, 2 * D_FF), F32)


def _unpack_states(st, fbuf):
    gl, hg, cm = {}, {}, {}
    for i, tile in enumerate(GH_TILES):
        s_t = st["s%d" % i]
        for qoff, dk, slot, kind, h in tile["heads"]:
            s = jnp.swapaxes(s_t[:, slot * SLOT:(slot + 1) * SLOT, qoff:qoff + dk], 1, 2)
            (gl if kind == "g" else hg)[h] = s
    for name, grp in zip(("ca", "cb"), ML_GROUPS):
        for qoff, slot, h in grp["heads"]:
            cm[h] = st[name][:, qoff:qoff + ML_DK, slot * SLOT:(slot + 1) * SLOT]
    n = st["nml"][:, 0, :]
    pos = {h: i for i, h in enumerate(ML_ORDER)}
    n_ml = jnp.stack([n[:, ML_DK * pos[h]:ML_DK * (pos[h] + 1)] for h in range(ML_H)], axis=1)
    m_ml = st["mml"][:, 0, :ML_H]
    buf_ml = _mqk_inv(st["cbuf"][:, SUBLANES - (ML_CONV - 1):, :])
    buf_ffn = fbuf[:, SUBLANES - (FFN_CONV - 1):, :]
    return (jnp.stack([gl[h] for h in range(GLA_H)], 1), jnp.stack([hg[h] for h in range(HG_H)], 1),
            jnp.stack([cm[h] for h in range(ML_H)], 1), n_ml, m_ml, buf_ml, buf_ffn)


def _block_sizes(B, T):
    if T % 512 == 0:
        return 1, 512, 512
    if T % 256 == 0:
        return 1, 256, 256
    bb = 4 if B % 4 == 0 else 1
    return bb, T, T


def _run_trunk(x, states, weights):
    B, T, _ = x.shape
    bb, tb_mix, tb_ffn = _block_sizes(B, T)
    new = [[] for _ in range(7)]
    for l in range(DEPTH):
        if states is None:
            st, fbuf = _zero_states(B)
        else:
            st, fbuf = _pack_states(*(s[l] for s in states))
        x, st = _mixer_call(x, weights[l], st, bb=bb, tb=tb_mix)
        x, fbuf = _ffn_call(x, weights[l], fbuf, bb=bb, tb=tb_ffn)
        for i, s in enumerate(_unpack_states(st, fbuf)):
            new[i].append(s)
    return x, [jnp.stack(s, axis=0) for s in new]


def kernel(x_prompt, x_sample, state_gla, state_hgrn, state_mlstm_C, state_mlstm_n,
           state_mlstm_m, cache_mlstm_conv, cache_ffn_conv, g_mix_pre, g_mix_post, g_ffn_pre,
           g_ffn_post, w_in, gla_w_gate, gla_b_gate, hgrn_lb, ml_conv_w, ml_conv_b, ml_b_i,
           ml_b_f, g_head, w_out, ffn_w_up, ffn_conv_w, ffn_conv_b, ffn_w_down):
    sm = jax.nn.softmax(hgrn_lb.astype(F32), axis=0)
    lb_all = jnp.cumsum(sm, axis=0) - sm[0:1]
    weights = [
        _layer_weights(l, lb_all, g_mix_pre, g_mix_post, g_ffn_pre, g_ffn_post, w_in, gla_w_gate,
                       gla_b_gate, ml_conv_w, ml_conv_b, ml_b_i, ml_b_f, g_head, w_out, ffn_w_up,
                       ffn_conv_w, ffn_conv_b, ffn_w_down)
        for l in range(DEPTH)]
    y_prompt, ps = _run_trunk(x_prompt, None, weights)
    sample_states = (state_gla, state_hgrn, state_mlstm_C, state_mlstm_n, state_mlstm_m,
                     cache_mlstm_conv, cache_ffn_conv)
    y_sample, ss = _run_trunk(x_sample, sample_states, weights)
    return (y_prompt, y_sample, *ps, *ss)
```
